```python
import jax, jax.numpy as jnp
from jax import lax
import numpy as np

D_MODEL = 4096
BATCH = 4
SEQ = 2048
DEPTH = 4
DEC_BATCH = 128
DEC_SEQ = 1
PAST_LEN = 16384
PAGE_SIZE = 128

W_A = D_MODEL // 4
H_A = 8
DK_A = W_A // H_A
DV_A = W_A // H_A
W_B = D_MODEL // 4
NB_B = 8
BS_B = W_B // NB_B
CONV_W = 4
LRU_C = 8.0
W_C = D_MODEL // 4
H_C = 8
DK_C = W_C // H_C
DV_C = W_C // H_C
N_BRANCH = 3
N_IN = 4 * W_A + 2 * W_B + 4 * W_C + N_BRANCH * D_MODEL
FF = 256 * ((8 * D_MODEL // 3 + 255) // 256)
EXPERT_FF = FF // 2
N_EXPERTS = 8
TOP_K = 2
N_DENSE = (DEPTH + 1) // 2
N_MOE = DEPTH // 2
CHUNK = 64
LN_EPS = 1e-5
RMS_EPS = 1e-6
ALPHA = (2 * DEPTH) ** 0.25
BETA = (8 * DEPTH) ** -0.25
ROPE_BASE = 10000.0

kernel_name = "hybrid_hgrn2_rglru_retention_step"


def layer_norm(x, g, b):
    xf = x.astype(jnp.float32)
    mu = jnp.mean(xf, axis=-1, keepdims=True)
    var = jnp.mean(jnp.square(xf - mu), axis=-1, keepdims=True)
    y = (xf - mu) * lax.rsqrt(var + LN_EPS) * g.astype(jnp.float32) + b.astype(jnp.float32)
    return y.astype(x.dtype)


def head_rms(o):
    return o * lax.rsqrt(jnp.mean(jnp.square(o), axis=-1, keepdims=True) + RMS_EPS)


def rotary(x, pos):
    half = x.shape[-1] // 2
    inv_freq = 1.0 / (ROPE_BASE ** jnp.linspace(0.0, 1.0, half, dtype=jnp.float32))
    ang = pos.astype(jnp.float32)[:, None] * inv_freq[None, :]
    cos = jnp.cos(ang)[None, :, None, :]
    sin = jnp.sin(ang)[None, :, None, :]
    x1, x2 = x[..., :half], x[..., half:]
    return jnp.concatenate([x1 * cos - x2 * sin, x1 * sin + x2 * cos], axis=-1)


def chunk_gla(q, k, v, log_f, s0):
    B, T, H, DK = q.shape
    DV = v.shape[-1]
    C = CHUNK if T % CHUNK == 0 else T
    n = T // C

    def to_chunks(a):
        return a.reshape(B, n, C, H, a.shape[-1]).transpose(1, 0, 3, 2, 4)

    causal = jnp.tril(jnp.ones((C, C), dtype=bool))[:, :, None]

    def step(s, inp):
        qi, ki, vi, gi = inp
        b = jnp.cumsum(gi, axis=2)
        rel = b[:, :, :, None, :] - b[:, :, None, :, :]
        decay = jnp.exp(jnp.where(causal, rel, -jnp.inf))
        scores = jnp.einsum('bhtsc,bhsc->bhts', decay * qi[:, :, :, None, :], ki)
        o = jnp.einsum('bhts,bhsv->bhtv', scores, vi) + jnp.einsum('bhtc,bhcv->bhtv', qi * jnp.exp(b), s)
        b_last = b[:, :, -1:, :]
        s_new = jnp.exp(b_last[:, :, 0, :])[..., None] * s + jnp.einsum('bhsc,bhsv->bhcv', ki * jnp.exp(b_last - b), vi)
        return s_new, o

    s_final, o = lax.scan(step, s0, (to_chunks(q), to_chunks(k), to_chunks(v), to_chunks(log_f)))
    o = o.transpose(1, 0, 3, 2, 4).reshape(B, T, H, DV)
    return o, s_final


def causal_conv(x, buf, w, b):
    T = x.shape[1]
    xp = jnp.concatenate([buf, x], axis=1)
    y = b + w[0] * xp[:, 0:T]
    for j in range(1, CONV_W):
        y = y + w[j] * xp[:, j:j + T]
    return y, xp[:, T:]


def _linear_combine(c1, c2):
    a1, b1 = c1
    a2, b2 = c2
    return a1 * a2, a2 * b1 + b2


def rg_lru(x, h0, wa, ba, wx, bx, lam):
    B, T, W = x.shape
    xb = x.reshape(B, T, NB_B, BS_B)
    r = jax.nn.sigmoid(jnp.einsum('btni,nij->btnj', xb, wa).reshape(B, T, W) + ba)
    i = jax.nn.sigmoid(jnp.einsum('btni,nij->btnj', xb, wx).reshape(B, T, W) + bx)
    log_a = -LRU_C * r * jax.nn.softplus(-lam)
    a = jnp.exp(log_a)
    u = jnp.sqrt(-jnp.expm1(2.0 * log_a)) * (i * x)
    u = u.at[:, 0].add(a[:, 0] * h0)
    _, h = lax.associative_scan(_linear_combine, (a, u), axis=1)
    return h, h[:, -1]


def token_mixer(x, s_hgrn, s_lru_h, s_lru_conv, s_ret, pos0,
                w_in, lb, conv_w, conv_b, lru_wa, lru_ba, lru_wx, lru_bx, lru_lambda, w_branch, w_out):
    B, T, _ = x.shape
    f32 = jnp.float32
    z = (x @ w_in).astype(f32)
    idx = np.cumsum([W_A] * 4 + [W_B] * 2 + [W_C] * 4).tolist()
    q_a, f_a, i_a, g_a, x_b, g_b, q_c, k_c, v_c, g_c, gate_logits = jnp.split(z, idx, axis=-1)

    f = lb.astype(f32) + (1.0 - lb.astype(f32)) * jax.nn.sigmoid(f_a)
    o_a, hgrn_new = chunk_gla(q_a.reshape(B, T, H_A, DK_A), (1.0 - f).reshape(B, T, H_A, DK_A),
                              i_a.reshape(B, T, H_A, DV_A), jnp.log(f).reshape(B, T, H_A, DK_A),
                              s_hgrn.astype(f32))
    y_a = head_rms(o_a).reshape(B, T, W_A) * jax.nn.silu(g_a)

    xc, conv_new = causal_conv(x_b, s_lru_conv.astype(f32), conv_w.astype(f32), conv_b.astype(f32))
    h, h_last = rg_lru(xc, s_lru_h.astype(f32), lru_wa.astype(f32), lru_ba.astype(f32),
                       lru_wx.astype(f32), lru_bx.astype(f32), lru_lambda.astype(f32))
    y_b = h * jax.nn.gelu(g_b)

    pos = pos0 + jnp.arange(T)
    q = rotary(q_c.reshape(B, T, H_C, DK_C), pos)
    k = rotary(k_c.reshape(B, T, H_C, DK_C), pos) * (DK_C ** -0.5)
    log_gamma = jnp.log1p(-jnp.power(2.0, -5.0 - jnp.arange(H_C, dtype=f32)))
    log_decay = jnp.broadcast_to(log_gamma[None, None, :, None], (B, T, H_C, DK_C))
    o_c, ret_new = chunk_gla(q, k, v_c.reshape(B, T, H_C, DV_C), log_decay, s_ret.astype(f32))
    y_c = head_rms(o_c).reshape(B, T, W_C) * jax.nn.silu(g_c)

    gates = jax.nn.sigmoid(gate_logits).reshape(B, T, N_BRANCH, D_MODEL).astype(x.dtype)
    branches = (y_a, y_b, y_c)
    merged = gates[:, :, 0] * (branches[0].astype(x.dtype) @ w_branch[0])
    for j in range(1, N_BRANCH):
        merged = merged + gates[:, :, j] * (branches[j].astype(x.dtype) @ w_branch[j])
    out = (merged @ w_out).astype(x.dtype)
    return out, (hgrn_new, h_last, conv_new, ret_new)


def swiglu(x, w1, w3, w2):
    return (jax.nn.silu(x @ w1) * (x @ w3)) @ w2


def moe_swiglu(x, router, w1, w3, w2):
    logits = (x @ router).astype(jnp.float32)
    top_v, top_i = lax.top_k(logits, TOP_K)
    wts = jax.nn.softmax(top_v, axis=-1)
    gate = jnp.sum(jax.nn.one_hot(top_i, N_EXPERTS, dtype=jnp.float32) * wts[..., None], axis=-2).astype(x.dtype)
    y = gate[..., 0:1] * swiglu(x, w1[0], w3[0], w2[0])
    for e in range(1, N_EXPERTS):
        y = y + gate[..., e:e + 1] * swiglu(x, w1[e], w3[e], w2[e])
    return y


def setup_inputs(seed: int = 0) -> dict:
    key = jax.random.key(seed)
    ks = jax.random.split(key, 32)
    f32 = jnp.float32

    def nrm(k, shape, scale):
        return jax.random.normal(k, shape, f32) * scale

    a_c = jax.random.uniform(ks[15], (DEPTH, W_B), f32, 0.9, 0.999)
    s = a_c ** (1.0 / LRU_C)
    lru_lambda = jnp.log(s) - jnp.log1p(-s)
    return {
        "x_prompt": nrm(ks[0], (BATCH, SEQ, D_MODEL), 1.0),
        "x_sample": nrm(ks[1], (DEC_BATCH, DEC_SEQ, D_MODEL), 1.0),
        "state_hgrn": nrm(ks[2], (DEPTH, DEC_BATCH, H_A, DK_A, DV_A), 0.3),
        "state_lru_h": nrm(ks[3], (DEPTH, DEC_BATCH, W_B), 0.5),
        "state_lru_conv": nrm(ks[4], (DEPTH, DEC_BATCH, CONV_W - 1, W_B), 1.0),
        "state_ret": nrm(ks[5], (DEPTH, DEC_BATCH, H_C, DK_C, DV_C), 0.3),
        "w_in": nrm(ks[6], (DEPTH, D_MODEL, N_IN), D_MODEL ** -0.5),
        "hgrn_lb_logits": nrm(ks[7], (DEPTH, W_A), 0.5),
        "lru_conv_w": nrm(ks[8], (DEPTH, CONV_W, W_B), CONV_W ** -0.5),
        "lru_conv_b": nrm(ks[9], (DEPTH, W_B), 0.02),
        "lru_wa": nrm(ks[10], (DEPTH, NB_B, BS_B, BS_B), BS_B ** -0.5),
        "lru_ba": nrm(ks[11], (DEPTH, W_B), 0.02),
        "lru_wx": nrm(ks[12], (DEPTH, NB_B, BS_B, BS_B), BS_B ** -0.5),
        "lru_bx": nrm(ks[13], (DEPTH, W_B), 0.02),
        "lru_lambda": lru_lambda,
        "w_branch": nrm(ks[14], (DEPTH, N_BRANCH, W_A, D_MODEL), BETA * W_A ** -0.5),
        "w_out": nrm(ks[16], (DEPTH, D_MODEL, D_MODEL), BETA * D_MODEL ** -0.5),
        "ln1_g": 1.0 + nrm(ks[17], (DEPTH, D_MODEL), 0.02),
        "ln1_b": nrm(ks[18], (DEPTH, D_MODEL), 0.02),
        "ln2_g": 1.0 + nrm(ks[19], (DEPTH, D_MODEL), 0.02),
        "ln2_b": nrm(ks[20], (DEPTH, D_MODEL), 0.02),
        "ffn_w1": nrm(ks[21], (N_DENSE, D_MODEL, FF), BETA * D_MODEL ** -0.5),
        "ffn_w3": nrm(ks[22], (N_DENSE, D_MODEL, FF), BETA * D_MODEL ** -0.5),
        "ffn_w2": nrm(ks[23], (N_DENSE, FF, D_MODEL), BETA * FF ** -0.5),
        "router": nrm(ks[24], (N_MOE, D_MODEL, N_EXPERTS), D_MODEL ** -0.5),
        "moe_w1": nrm(ks[25], (N_MOE, N_EXPERTS, D_MODEL, EXPERT_FF), BETA * D_MODEL ** -0.5),
        "moe_w3": nrm(ks[26], (N_MOE, N_EXPERTS, D_MODEL, EXPERT_FF), BETA * D_MODEL ** -0.5),
        "moe_w2": nrm(ks[27], (N_MOE, N_EXPERTS, EXPERT_FF, D_MODEL), BETA * EXPERT_FF ** -0.5),
    }


def _stack(states, j):
    return jnp.stack([s[j] for s in states], axis=0)


def reference(x_prompt, x_sample, state_hgrn, state_lru_h, state_lru_conv, state_ret,
              w_in, hgrn_lb_logits, lru_conv_w, lru_conv_b, lru_wa, lru_ba, lru_wx, lru_bx, lru_lambda,
              w_branch, w_out, ln1_g, ln1_b, ln2_g, ln2_b,
              ffn_w1, ffn_w3, ffn_w2, router, moe_w1, moe_w3, moe_w2):
    f32 = jnp.float32
    p = jax.nn.softmax(hgrn_lb_logits.astype(f32), axis=0)
    lb_all = jnp.concatenate([jnp.zeros_like(p[:1]), jnp.cumsum(p, axis=0)[:-1]], axis=0)

    nb = x_prompt.shape[0]
    zero_states = (jnp.zeros((nb, H_A, DK_A, DV_A), f32), jnp.zeros((nb, W_B), f32),
                   jnp.zeros((nb, CONV_W - 1, W_B), f32), jnp.zeros((nb, H_C, DK_C, DV_C), f32))

    def layer(x, states, l, pos0):
        mix, new_states = token_mixer(x, states[0], states[1], states[2], states[3], pos0,
                                      w_in[l], lb_all[l], lru_conv_w[l], lru_conv_b[l], lru_wa[l], lru_ba[l],
                                      lru_wx[l], lru_bx[l], lru_lambda[l], w_branch[l], w_out[l])
        x = layer_norm(ALPHA * x + mix, ln1_g[l], ln1_b[l])
        if l % 2 == 0:
            ffn = swiglu(x, ffn_w1[l // 2], ffn_w3[l // 2], ffn_w2[l // 2])
        else:
            ffn = moe_swiglu(x, router[l // 2], moe_w1[l // 2], moe_w3[l // 2], moe_w2[l // 2])
        x = layer_norm(ALPHA * x + ffn.astype(x.dtype), ln2_g[l], ln2_b[l])
        return x, new_states

    xp, xs = x_prompt, x_sample
    new_p, new_s = [], []
    for l in range(DEPTH):
        xp, sp = layer(xp, zero_states, l, 0)
        xs, ss = layer(xs, (state_hgrn[l], state_lru_h[l], state_lru_conv[l], state_ret[l]), l, PAST_LEN)
        new_p.append(sp)
        new_s.append(ss)

    hgrn_p, lru_h_p, lru_conv_p, ret_p = _stack(new_p, 0), _stack(new_p, 1), _stack(new_p, 2), _stack(new_p, 3)
    hgrn_s, lru_h_s, lru_conv_s, ret_s = _stack(new_s, 0), _stack(new_s, 1), _stack(new_s, 2), _stack(new_s, 3)
    return (xp, xs, hgrn_p, lru_h_p, lru_conv_p, ret_p, hgrn_s, lru_h_s, lru_conv_s, ret_s)
```

```python
import functools

import jax
import jax.numpy as jnp
from jax import lax
from jax.experimental import pallas as pl
from jax.experimental.pallas import tpu as pltpu

F32 = jnp.float32
BF16 = jnp.bfloat16

LANE = 128
SUBLANE = 8
MIB = 1024 * 1024

D_MODEL = 4096
DEPTH = 4
N_HEADS = 8
HEAD = 128
W_BR = 1024
CONV_W = 4
LRU_C = 8.0
N_BRANCH = 3
N_IN = 10 * W_BR + N_BRANCH * D_MODEL
N_EXPERTS = 8
LN_EPS = 1e-5
RMS_EPS = 1e-6
ALPHA = (2 * DEPTH) ** 0.25
ROPE_BASE = 10000.0
PAST_LEN = 16384

SEG_QA, SEG_FA, SEG_IA, SEG_GA, SEG_XB, SEG_GB, SEG_QC, SEG_KC, SEG_VC, SEG_GC = range(10)
GATE_COL0 = 10 * W_BR

GLA_CHUNK = 64
GLA_SUB = 16
NEG_BIG = -1e30


def _params(sem, vmem_mib):
    return pltpu.CompilerParams(dimension_semantics=sem, vmem_limit_bytes=vmem_mib * MIB)


def _dot(a, b):
    return jnp.dot(a, b, preferred_element_type=F32)


def _dot_nt(a, b):
    return lax.dot_general(a, b, (((1,), (1,)), ((), ())), preferred_element_type=F32)


def _dot_tn(a, b):
    return lax.dot_general(a, b, (((0,), (0,)), ((), ())), preferred_element_type=F32)


def _sigmoid(x):
    return 1.0 / (1.0 + jnp.exp(-x))


def _silu(x):
    return x * _sigmoid(x)


def _gelu_tanh(x):
    return 0.5 * x * (1.0 + jnp.tanh(0.7978845608028654 * (x + 0.044715 * (x * x * x))))


def _head_rms(o):
    return o * lax.rsqrt(jnp.mean(o * o, axis=-1, keepdims=True) + RMS_EPS)


def _lb_kernel(g_ref, o_ref):
    g = g_ref[...]
    rows = [g[j:j + 1, :] for j in range(DEPTH)]
    m = rows[0]
    for r in rows[1:]:
        m = jnp.maximum(m, r)
    e = [jnp.exp(r - m) for r in rows]
    tot = e[0]
    for x in e[1:]:
        tot = tot + x
    acc = jnp.zeros_like(m)
    for j in range(DEPTH):
        o_ref[j:j + 1, :] = acc
        acc = acc + e[j] / tot


def hgrn_lower_bounds(logits):
    return pl.pallas_call(_lb_kernel, out_shape=jax.ShapeDtypeStruct(logits.shape, F32), name="hgrn_lb")(logits)


def _mm_kernel(x_ref, w_ref, o_ref, wb_ref):
    @pl.when(pl.program_id(1) == 0)
    def _():
        wb_ref[...] = w_ref[...].astype(BF16)

    o_ref[...] = _dot(x_ref[...], wb_ref[...]).astype(o_ref.dtype)


def matmul(x, w_stack, layer, tm, tn, out_dtype=F32, name="matmul"):
    M, K = x.shape
    N = w_stack.shape[-1]
    assert M % tm == 0 and N % tn == 0
    return pl.pallas_call(
        _mm_kernel,
        grid=(N // tn, M // tm),
        in_specs=[pl.BlockSpec((tm, K), lambda j, i: (i, 0)),
                  pl.BlockSpec((None, K, tn), lambda j, i: (layer, 0, j))],
        out_specs=pl.BlockSpec((tm, tn), lambda j, i: (i, j)),
        out_shape=jax.ShapeDtypeStruct((M, N), out_dtype),
        scratch_shapes=[pltpu.VMEM((K, tn), BF16)],
        compiler_params=_params(("arbitrary", "arbitrary"), 56),
        name=name,
    )(x, w_stack)


def _swiglu_up_kernel(x_ref, w1_ref, w3_ref, o_ref, wb_ref, *, tn):
    @pl.when(pl.program_id(1) == 0)
    def _():
        wb_ref[:, :tn] = w1_ref[...].astype(BF16)
        wb_ref[:, tn:] = w3_ref[...].astype(BF16)

    ab = _dot(x_ref[...], wb_ref[...])
    o_ref[...] = (_silu(ab[:, :tn]) * ab[:, tn:]).astype(o_ref.dtype)


def swiglu_up(x, w1, w3, idx, tm, tn, name="swiglu_up"):
    M, K = x.shape
    FF = w1.shape[-1]
    nlead = len(idx)
    wspec = pl.BlockSpec((None,) * nlead + (K, tn), lambda j, i: idx + (0, j))
    return pl.pallas_call(
        functools.partial(_swiglu_up_kernel, tn=tn),
        grid=(pl.cdiv(FF, tn), M // tm),
        in_specs=[pl.BlockSpec((tm, K), lambda j, i: (i, 0)), wspec, wspec],
        out_specs=pl.BlockSpec((tm, tn), lambda j, i: (i, j)),
        out_shape=jax.ShapeDtypeStruct((M, FF), BF16),
        scratch_shapes=[pltpu.VMEM((K, 2 * tn), BF16)],
        compiler_params=_params(("arbitrary", "arbitrary"), 56),
        name=name,
    )(x, w1, w3)


def _down_kernel(h_ref, w_ref, o_ref, wb_ref):
    @pl.when(pl.program_id(1) == 0)
    def _():
        wb_ref[...] = w_ref[...].astype(BF16)

    o_ref[...] = _dot(h_ref[...], wb_ref[...])


def _down_acc_kernel(h_ref, w_ref, p_ref, o_ref, wb_ref):
    @pl.when(pl.program_id(1) == 0)
    def _():
        wb_ref[...] = w_ref[...].astype(BF16)

    o_ref[...] = p_ref[...] + _dot(h_ref[...], wb_ref[...])


def _down_gate_kernel(h_ref, w_ref, g_ref, p_ref, o_ref, wb_ref, *, e, first):
    @pl.when(pl.program_id(1) == 0)
    def _():
        wb_ref[...] = w_ref[...].astype(BF16)

    y = g_ref[:, e:e + 1] * _dot(h_ref[...], wb_ref[...])
    o_ref[...] = y if first else p_ref[...] + y


def down_proj(h, w, idx, kblk, tm, tn, prev=None, gate=None, expert=0, name="down"):
    M = h.shape[0]
    Kb = h.shape[1] // (2 if kblk is not None else 1)
    kb = 0 if kblk is None else kblk
    N = w.shape[-1]
    nlead = len(idx)
    h_spec = pl.BlockSpec((tm, Kb), lambda j, i: (i, kb))
    w_spec = pl.BlockSpec((None,) * nlead + (Kb, tn), lambda j, i: idx + (kb, j))
    o_spec = pl.BlockSpec((tm, tn), lambda j, i: (i, j))
    args, specs, alias = [h, w], [h_spec, w_spec], {}
    if gate is not None:
        first = prev is None
        if first:
            prev = jnp.zeros((SUBLANE, LANE), F32)
            p_spec = pl.BlockSpec((SUBLANE, LANE), lambda j, i: (0, 0))
        else:
            p_spec = o_spec
            alias = {3: 0}
        body = functools.partial(_down_gate_kernel, e=expert, first=first)
        args += [gate, prev]
        specs += [pl.BlockSpec((tm, N_EXPERTS), lambda j, i: (i, 0)), p_spec]
    elif prev is not None:
        body = _down_acc_kernel
        args.append(prev)
        specs.append(o_spec)
        alias = {2: 0}
    else:
        body = _down_kernel
    return pl.pallas_call(
        body,
        grid=(N // tn, M // tm),
        in_specs=specs,
        out_specs=o_spec,
        out_shape=jax.ShapeDtypeStruct((M, N), F32),
        scratch_shapes=[pltpu.VMEM((Kb, tn), BF16)],
        input_output_aliases=alias,
        compiler_params=_params(("arbitrary", "arbitrary"), 56),
        name=name,
    )(*args)


def _merge_kernel(ya_ref, yb_ref, yc_ref, w_ref, ga_ref, gb_ref, gc_ref, o_ref, wb_ref):
    @pl.when(pl.program_id(1) == 0)
    def _():
        wb_ref[...] = w_ref[...].astype(BF16)

    acc = _sigmoid(ga_ref[...]) * _dot(ya_ref[...], wb_ref[0])
    acc = acc + _sigmoid(gb_ref[...]) * _dot(yb_ref[...], wb_ref[1])
    acc = acc + _sigmoid(gc_ref[...]) * _dot(yc_ref[...], wb_ref[2])
    o_ref[...] = acc.astype(o_ref.dtype)


def branch_merge(ya, yb, yc, w_branch, layer, z, tm, tn):
    M = ya.shape[0]
    nt = D_MODEL // tn
    g0 = GATE_COL0 // tn
    y_spec = pl.BlockSpec((tm, W_BR), lambda j, i: (i, 0))

    def gate_spec(b):
        return pl.BlockSpec((tm, tn), lambda j, i: (i, g0 + b * nt + j))

    return pl.pallas_call(
        _merge_kernel,
        grid=(nt, M // tm),
        in_specs=[y_spec, y_spec, y_spec,
                  pl.BlockSpec((None, N_BRANCH, W_BR, tn), lambda j, i: (layer, 0, 0, j)),
                  gate_spec(0), gate_spec(1), gate_spec(2)],
        out_specs=pl.BlockSpec((tm, tn), lambda j, i: (i, j)),
        out_shape=jax.ShapeDtypeStruct((M, D_MODEL), BF16),
        scratch_shapes=[pltpu.VMEM((N_BRANCH, W_BR, tn), BF16)],
        compiler_params=_params(("arbitrary", "arbitrary"), 48),
        name="branch_merge",
    )(ya, yb, yc, w_branch, z, z, z)


def _ln_kernel(x_ref, y_ref, g_ref, b_ref, o_ref, ob_ref):
    v = ALPHA * x_ref[...] + y_ref[...]
    mu = jnp.mean(v, axis=-1, keepdims=True)
    c = v - mu
    var = jnp.mean(c * c, axis=-1, keepdims=True)
    o = c * lax.rsqrt(var + LN_EPS) * g_ref[...] + b_ref[...]
    o_ref[...] = o
    ob_ref[...] = o.astype(BF16)


def residual_ln(x, y, g, b, layer, tm):
    M = x.shape[0]
    row = pl.BlockSpec((tm, D_MODEL), lambda i: (i, 0))
    vec = pl.BlockSpec((None, 1, D_MODEL), lambda i: (layer, 0, 0))
    return pl.pallas_call(
        _ln_kernel,
        grid=(M // tm,),
        in_specs=[row, row, vec, vec],
        out_specs=[row, row],
        out_shape=[jax.ShapeDtypeStruct((M, D_MODEL), F32), jax.ShapeDtypeStruct((M, D_MODEL), BF16)],
        compiler_params=_params(("arbitrary",), 48),
        name="residual_ln",
    )(x, y, g.reshape(DEPTH, 1, D_MODEL), b.reshape(DEPTH, 1, D_MODEL))


def _cast_kernel(x_ref, o_ref):
    o_ref[...] = x_ref[...].astype(BF16)


def to_bf16(x, tm):
    M, N = x.shape
    spec = pl.BlockSpec((tm, N), lambda i: (i, 0))
    return pl.pallas_call(_cast_kernel, grid=(M // tm,), in_specs=[spec], out_specs=spec,
                          out_shape=jax.ShapeDtypeStruct((M, N), BF16),
                          compiler_params=_params(("arbitrary",), 32), name="to_bf16")(x)


def _top1(cols):
    m = cols[0]
    for c in cols[1:]:
        m = jnp.maximum(m, c)
    idx = jnp.full(m.shape, N_EXPERTS, jnp.int32)
    for e in reversed(range(N_EXPERTS)):
        idx = jnp.where(cols[e] == m, e, idx)
    return m, idx


def _router_kernel(x_ref, rt_ref, o_ref):
    x = x_ref[...]
    cols = [jnp.sum(x * rt_ref[e:e + 1, :], axis=-1, keepdims=True) for e in range(N_EXPERTS)]
    m1, i1 = _top1(cols)
    m2, i2 = _top1([jnp.where(i1 == e, -jnp.inf, cols[e]) for e in range(N_EXPERTS)])
    e2 = jnp.exp(m2 - m1)
    den = 1.0 + e2
    ids = lax.broadcasted_iota(jnp.int32, o_ref.shape, 1)
    o_ref[...] = jnp.where(ids == i1, 1.0 / den, 0.0) + jnp.where(ids == i2, e2 / den, 0.0)


def router_gates(x, router_t, idx, tm):
    M = x.shape[0]
    return pl.pallas_call(
        _router_kernel,
        grid=(M // tm,),
        in_specs=[pl.BlockSpec((tm, D_MODEL), lambda i: (i, 0)),
                  pl.BlockSpec((None, N_EXPERTS, D_MODEL), lambda i: (idx, 0, 0))],
        out_specs=pl.BlockSpec((tm, N_EXPERTS), lambda i: (i, 0)),
        out_shape=jax.ShapeDtypeStruct((M, N_EXPERTS), F32),
        compiler_params=_params(("arbitrary",), 48),
        name="router",
    )(x, router_t)


def _hgrn_prompt_kernel(q_ref, f_ref, i_ref, g_ref, lb_ref, y_ref, s_ref, st_ref, *, tb):
    t = pl.program_id(2)

    @pl.when(t == 0)
    def _():
        st_ref[...] = jnp.zeros_like(st_ref)

    lb = lb_ref[...]
    C, SB = GLA_CHUNK, GLA_SUB
    r64 = lax.broadcasted_iota(jnp.int32, (C, C), 0)
    c64 = lax.broadcasted_iota(jnp.int32, (C, C), 1)
    tril = (r64 >= c64).astype(F32)
    rid = lax.broadcasted_iota(jnp.int32, (SB, HEAD), 0)
    ridc = lax.broadcasted_iota(jnp.int32, (C, HEAD), 0)

    for c in range(tb // C):
        rows = pl.ds(c * C, C)
        q = q_ref[rows, :]
        v = i_ref[rows, :]
        f = lb + (1.0 - lb) * _sigmoid(f_ref[rows, :])
        k = 1.0 - f
        b = jnp.dot(tril, jnp.log(f), preferred_element_type=F32, precision=lax.Precision.HIGHEST)
        st = st_ref[...]
        vb = v.astype(BF16)
        o_inter = _dot_nt((q * jnp.exp(b)).astype(BF16), st.astype(BF16))
        outs = []
        for i in range(C // SB):
            lo = i * SB
            qi, bi, ki, vi = q[lo:lo + SB], b[lo:lo + SB], k[lo:lo + SB], v[lo:lo + SB]
            acc = o_inter[lo:lo + SB]
            if i > 0:
                beta = b[lo - 1:lo]
                qt = (qi * jnp.exp(bi - beta)).astype(BF16)
                kt = (k * jnp.exp(jnp.where(ridc < lo, beta - b, NEG_BIG))).astype(BF16)
                acc = acc + _dot(_dot_nt(qt, kt).astype(BF16), vb)
            for s in range(SB):
                d = jnp.where(rid >= s, bi - bi[s:s + 1], NEG_BIG)
                w = qi * jnp.exp(d) * ki[s:s + 1]
                acc = acc + jnp.sum(w, axis=1, keepdims=True) * vi[s:s + 1]
            outs.append(acc)
        o = jnp.concatenate(outs, axis=0)
        bl = b[C - 1:C]
        kt = (k * jnp.exp(bl - b)).astype(BF16)
        st_ref[...] = jnp.exp(bl) * st + _dot_tn(vb, kt)
        y_ref[rows, :] = (_head_rms(o) * _silu(g_ref[rows, :])).astype(BF16)

    @pl.when(t == pl.num_programs(2) - 1)
    def _():
        s_ref[...] = st_ref[...].T


def hgrn_prompt(z, lb, nb, T, tb):
    nt = T // tb

    def seg(s):
        return pl.BlockSpec((tb, HEAD), lambda b, h, t: (b * nt + t, s * N_HEADS + h))

    return pl.pallas_call(
        functools.partial(_hgrn_prompt_kernel, tb=tb),
        grid=(nb, N_HEADS, nt),
        in_specs=[seg(SEG_QA), seg(SEG_FA), seg(SEG_IA), seg(SEG_GA),
                  pl.BlockSpec((1, HEAD), lambda b, h, t: (0, h))],
        out_specs=[pl.BlockSpec((tb, HEAD), lambda b, h, t: (b * nt + t, h)),
                   pl.BlockSpec((None, None, HEAD, HEAD), lambda b, h, t: (b, h, 0, 0))],
        out_shape=[jax.ShapeDtypeStruct((nb * T, W_BR), BF16),
                   jax.ShapeDtypeStruct((nb, N_HEADS, HEAD, HEAD), F32)],
        scratch_shapes=[pltpu.VMEM((HEAD, HEAD), F32)],
        compiler_params=_params(("arbitrary", "arbitrary", "arbitrary"), 32),
        name="hgrn_prompt",
    )(z, z, z, z, lb)


def _rotate(x, cosf, sinf):
    return x * cosf + pltpu.roll(x, HEAD // 2, 1) * sinf


def _ret_prompt_kernel(q_ref, k_ref, v_ref, g_ref, cos_ref, sin_ref, lg_ref, y_ref, s_ref, st_ref, *, tb):
    t = pl.program_id(2)

    @pl.when(t == 0)
    def _():
        st_ref[...] = jnp.zeros_like(st_ref)

    lg_w = lg_ref[...]
    lg = lg_w[:, :HEAD]
    cosf, sinf = cos_ref[...], sin_ref[...]
    qr = _rotate(q_ref[...], cosf, sinf)
    kr = _rotate(k_ref[...], cosf, sinf) * (HEAD ** -0.5)
    vb = v_ref[...].astype(BF16)
    tpos = lax.broadcasted_iota(jnp.int32, (tb, HEAD), 0).astype(F32)
    r = lax.broadcasted_iota(jnp.int32, (tb, tb), 0)
    c = lax.broadcasted_iota(jnp.int32, (tb, tb), 1)
    dmat = jnp.where(r >= c, jnp.exp((r - c).astype(F32) * lg_w), 0.0)
    st = st_ref[...]
    scores = _dot_nt(qr.astype(BF16), kr.astype(BF16)) * dmat
    o = _dot(scores.astype(BF16), vb)
    o = o + _dot_nt((qr * jnp.exp((tpos + 1.0) * lg)).astype(BF16), st.astype(BF16))
    kt = (kr * jnp.exp((tb - 1.0 - tpos) * lg)).astype(BF16)
    st_ref[...] = jnp.exp(float(tb) * lg) * st + _dot_tn(vb, kt)
    y_ref[...] = (_head_rms(o) * _silu(g_ref[...])).astype(BF16)

    @pl.when(t == pl.num_programs(2) - 1)
    def _():
        s_ref[...] = st_ref[...].T


def _rope_tables(pos):
    half = HEAD // 2
    inv_freq = 1.0 / (ROPE_BASE ** jnp.linspace(0.0, 1.0, half, dtype=F32))
    ang = pos.astype(F32)[:, None] * inv_freq[None, :]
    cos, sin = jnp.cos(ang), jnp.sin(ang)
    return jnp.concatenate([cos, cos], axis=-1), jnp.concatenate([-sin, sin], axis=-1)


def _log_gamma():
    return jnp.log1p(-jnp.power(2.0, -5.0 - jnp.arange(N_HEADS, dtype=F32)))


def ret_prompt(z, nb, T, tb):
    nt = T // tb
    cosf, sinf = _rope_tables(jnp.arange(T))
    lg = jnp.broadcast_to(_log_gamma()[:, None, None], (N_HEADS, 1, tb))

    def seg(s):
        return pl.BlockSpec((tb, HEAD), lambda b, h, t: (b * nt + t, s * N_HEADS + h))

    tab = pl.BlockSpec((tb, HEAD), lambda b, h, t: (t, 0))
    return pl.pallas_call(
        functools.partial(_ret_prompt_kernel, tb=tb),
        grid=(nb, N_HEADS, nt),
        in_specs=[seg(SEG_QC), seg(SEG_KC), seg(SEG_VC), seg(SEG_GC), tab, tab,
                  pl.BlockSpec((None, 1, tb), lambda b, h, t: (h, 0, 0))],
        out_specs=[pl.BlockSpec((tb, HEAD), lambda b, h, t: (b * nt + t, h)),
                   pl.BlockSpec((None, None, HEAD, HEAD), lambda b, h, t: (b, h, 0, 0))],
        out_shape=[jax.ShapeDtypeStruct((nb * T, W_BR), BF16),
                   jax.ShapeDtypeStruct((nb, N_HEADS, HEAD, HEAD), F32)],
        scratch_shapes=[pltpu.VMEM((HEAD, HEAD), F32)],
        compiler_params=_params(("arbitrary", "arbitrary", "arbitrary"), 32),
        name="ret_prompt",
    )(z, z, z, z, cosf, sinf, lg)


def _lru_gates(xc, wa_ref, ba_ref, wx_ref, bx_ref, lam_ref):
    xcb = xc.astype(BF16)
    ra, ia = [], []
    for n in range(W_BR // HEAD):
        xn = xcb[:, n * HEAD:(n + 1) * HEAD]
        ra.append(_dot(xn, wa_ref[n].astype(BF16)))
        ia.append(_dot(xn, wx_ref[n].astype(BF16)))
    r = _sigmoid(jnp.concatenate(ra, axis=1) + ba_ref[...])
    i = _sigmoid(jnp.concatenate(ia, axis=1) + bx_ref[...])
    nlam = -lam_ref[...]
    softplus = jnp.maximum(nlam, 0.0) + jnp.log1p(jnp.exp(-jnp.abs(nlam)))
    log_a = -LRU_C * r * softplus
    a = jnp.exp(log_a)
    th = jnp.tanh(log_a)
    u = jnp.sqrt(-2.0 * th / (1.0 - th)) * (i * xc)
    return a, u


def _lru_prompt_kernel(x_ref, g_ref, cw_ref, cb_ref, wa_ref, ba_ref, wx_ref, bx_ref, lam_ref,
                       y_ref, h_ref, conv_ref, prev_ref, hc_ref, *, tb):
    t = pl.program_id(1)

    @pl.when(t == 0)
    def _():
        prev_ref[...] = jnp.zeros_like(prev_ref)
        hc_ref[...] = jnp.zeros_like(hc_ref)

    x = x_ref[...]
    prev = prev_ref[...]
    rid = lax.broadcasted_iota(jnp.int32, (tb, W_BR), 0)
    rid8 = lax.broadcasted_iota(jnp.int32, (SUBLANE, W_BR), 0)
    cw = cw_ref[...]
    xc = cb_ref[...] + cw[CONV_W - 1:CONV_W] * x
    for j in range(1, CONV_W):
        xr = pltpu.roll(x, j, 0)
        head = jnp.where(rid8 < j, pltpu.roll(prev, j, 0), xr[:SUBLANE])
        xc = xc + cw[CONV_W - 1 - j:CONV_W - j] * jnp.concatenate([head, xr[SUBLANE:]], axis=0)
    a, u = _lru_gates(xc, wa_ref, ba_ref, wx_ref, bx_ref, lam_ref)
    d = 1
    while d < tb:
        keep = rid >= d
        a_s = jnp.where(keep, pltpu.roll(a, d, 0), 1.0)
        u_s = jnp.where(keep, pltpu.roll(u, d, 0), 0.0)
        u = a * u_s + u
        a = a * a_s
        d *= 2
    h = a * hc_ref[0:1, :] + u
    y_ref[...] = (h * _gelu_tanh(g_ref[...])).astype(BF16)
    hc_ref[0:1, :] = h[tb - 1:tb]
    prev_ref[...] = x[tb - SUBLANE:tb]

    @pl.when(t == pl.num_programs(1) - 1)
    def _():
        h_ref[...] = h[tb - 1:tb]
        conv_ref[...] = x[tb - (CONV_W - 1):tb]


def _lru_weight_specs(layer):
    def vec():
        return pl.BlockSpec((None, 1, W_BR), lambda *_: (layer, 0, 0))

    def blk():
        return pl.BlockSpec((None, W_BR // HEAD, HEAD, HEAD), lambda *_: (layer, 0, 0, 0))

    return [pl.BlockSpec((None, CONV_W, W_BR), lambda *_: (layer, 0, 0)), vec(), blk(), vec(), blk(), vec(), vec()]


def _lru_weight_args(p):
    v = lambda a: a.reshape(DEPTH, 1, W_BR)
    return [p["lru_conv_w"], v(p["lru_conv_b"]), p["lru_wa"], v(p["lru_ba"]), p["lru_wx"], v(p["lru_bx"]),
            v(p["lru_lambda"])]


def lru_prompt(z, p, layer, nb, T, tb):
    nt = T // tb
    outs = pl.pallas_call(
        functools.partial(_lru_prompt_kernel, tb=tb),
        grid=(nb, nt),
        in_specs=[pl.BlockSpec((tb, W_BR), lambda b, t: (b * nt + t, SEG_XB)),
                  pl.BlockSpec((tb, W_BR), lambda b, t: (b * nt + t, SEG_GB))] + _lru_weight_specs(layer),
        out_specs=[pl.BlockSpec((tb, W_BR), lambda b, t: (b * nt + t, 0)),
                   pl.BlockSpec((None, 1, W_BR), lambda b, t: (b, 0, 0)),
                   pl.BlockSpec((None, CONV_W - 1, W_BR), lambda b, t: (b, 0, 0))],
        out_shape=[jax.ShapeDtypeStruct((nb * T, W_BR), BF16),
                   jax.ShapeDtypeStruct((nb, 1, W_BR), F32),
                   jax.ShapeDtypeStruct((nb, CONV_W - 1, W_BR), F32)],
        scratch_shapes=[pltpu.VMEM((SUBLANE, W_BR), F32), pltpu.VMEM((SUBLANE, W_BR), F32)],
        compiler_params=_params(("arbitrary", "arbitrary"), 48),
        name="lru_prompt",
    )(z, z, *_lru_weight_args(p))
    return outs[0], outs[1].reshape(nb, W_BR), outs[2]


def _lru_sample_kernel(x_ref, g_ref, buf_ref, h0_ref, cw_ref, cb_ref, wa_ref, ba_ref, wx_ref, bx_ref, lam_ref,
                       y_ref, h_ref, conv_ref):
    x = x_ref[...]
    cw = cw_ref[...]
    xc = cb_ref[...] + cw[CONV_W - 1:CONV_W] * x
    for j in range(CONV_W - 1):
        xc = xc + cw[j:j + 1] * buf_ref[j]
    a, u = _lru_gates(xc, wa_ref, ba_ref, wx_ref, bx_ref, lam_ref)
    h = a * h0_ref[...] + u
    y_ref[...] = (h * _gelu_tanh(g_ref[...])).astype(BF16)
    h_ref[...] = h
    for j in range(CONV_W - 2):
        conv_ref[j] = buf_ref[j + 1]
    conv_ref[CONV_W - 2] = x


def lru_sample(z, p, layer, row_blk, ns, buf_t, h0):
    full3 = pl.BlockSpec((CONV_W - 1, ns, W_BR), lambda i: (0, 0, 0))
    full2 = pl.BlockSpec((ns, W_BR), lambda i: (0, 0))
    return pl.pallas_call(
        _lru_sample_kernel,
        grid=(1,),
        in_specs=[pl.BlockSpec((ns, W_BR), lambda i: (row_blk, SEG_XB)),
                  pl.BlockSpec((ns, W_BR), lambda i: (row_blk, SEG_GB)), full3, full2] + _lru_weight_specs(layer),
        out_specs=[full2, full2, full3],
        out_shape=[jax.ShapeDtypeStruct((ns, W_BR), BF16), jax.ShapeDtypeStruct((ns, W_BR), F32),
                   jax.ShapeDtypeStruct((CONV_W - 1, ns, W_BR), F32)],
        compiler_params=_params(("arbitrary",), 32),
        name="lru_sample",
    )(z, z, buf_t, h0, *_lru_weight_args(p))


GRP = SUBLANE


def _channel_major(zs, seg):
    ns = zs.shape[0]
    x = zs[:, seg * W_BR:(seg + 1) * W_BR].reshape(ns // GRP, GRP, N_HEADS, HEAD)
    return x.transpose(0, 2, 3, 1)


def _hgrn_sample_kernel(q_ref, f_ref, v_ref, g_ref, lb_ref, s_ref, y_ref, so_ref, o_scr):
    for h in range(N_HEADS):
        lb = lb_ref[h]
        f = lb + (1.0 - lb) * _sigmoid(f_ref[h])
        k = 1.0 - f
        q = q_ref[h]
        for j in range(GRP):
            vrow = v_ref[j:j + 1, h * HEAD:(h + 1) * HEAD]
            sn = f[:, j:j + 1] * s_ref[j, h] + k[:, j:j + 1] * vrow
            so_ref[j, h] = sn
            o_scr[j:j + 1, h * HEAD:(h + 1) * HEAD] = jnp.sum(q[:, j:j + 1] * sn, axis=0, keepdims=True)
    for h in range(N_HEADS):
        cols = slice(h * HEAD, (h + 1) * HEAD)
        y_ref[:, cols] = (_head_rms(o_scr[:, cols]) * _silu(g_ref[:, cols])).astype(BF16)


def hgrn_sample(z, zs, lb, state, row0):
    ns = zs.shape[0]
    cm = pl.BlockSpec((None, N_HEADS, HEAD, GRP), lambda g: (g, 0, 0, 0))
    st = pl.BlockSpec((GRP, N_HEADS, HEAD, HEAD), lambda g: (g, 0, 0, 0))
    rb = row0 // GRP

    def rows(seg):
        return pl.BlockSpec((GRP, W_BR), lambda g: (rb + g, seg))

    return pl.pallas_call(
        _hgrn_sample_kernel,
        grid=(ns // GRP,),
        in_specs=[cm, cm, rows(SEG_IA), rows(SEG_GA),
                  pl.BlockSpec((N_HEADS, HEAD, 1), lambda g: (0, 0, 0)), st],
        out_specs=[pl.BlockSpec((GRP, W_BR), lambda g: (g, 0)), st],
        out_shape=[jax.ShapeDtypeStruct((ns, W_BR), BF16), jax.ShapeDtypeStruct(state.shape, F32)],
        scratch_shapes=[pltpu.VMEM((GRP, W_BR), F32)],
        compiler_params=_params(("arbitrary",), 48),
        name="hgrn_sample",
    )(_channel_major(zs, SEG_QA), _channel_major(zs, SEG_FA), z, z, lb.reshape(N_HEADS, HEAD, 1), state)


def _ret_sample_kernel(q_ref, k_ref, v_ref, g_ref, cos_ref, sin_ref, gam_ref, s_ref, y_ref, so_ref, o_scr):
    half = HEAD // 2
    cosc, sinc = cos_ref[...], sin_ref[...]

    def rot(x):
        x1, x2 = x[:half], x[half:]
        return jnp.concatenate([x1 * cosc - x2 * sinc, x1 * sinc + x2 * cosc], axis=0)

    for h in range(N_HEADS):
        q = rot(q_ref[h])
        k = rot(k_ref[h]) * (HEAD ** -0.5)
        gam = gam_ref[h]
        for j in range(GRP):
            vrow = v_ref[j:j + 1, h * HEAD:(h + 1) * HEAD]
            sn = gam * s_ref[j, h] + k[:, j:j + 1] * vrow
            so_ref[j, h] = sn
            o_scr[j:j + 1, h * HEAD:(h + 1) * HEAD] = jnp.sum(q[:, j:j + 1] * sn, axis=0, keepdims=True)
    for h in range(N_HEADS):
        cols = slice(h * HEAD, (h + 1) * HEAD)
        y_ref[:, cols] = (_head_rms(o_scr[:, cols]) * _silu(g_ref[:, cols])).astype(BF16)


def ret_sample(z, zs, state, row0):
    ns = zs.shape[0]
    half = HEAD // 2
    cosf, sinf = _rope_tables(jnp.full((1,), PAST_LEN))
    cosc = cosf[0, :half].reshape(half, 1)
    sinc = sinf[0, half:].reshape(half, 1)
    gam = jnp.broadcast_to(jnp.exp(_log_gamma())[:, None, None], (N_HEADS, 1, HEAD))
    cm = pl.BlockSpec((None, N_HEADS, HEAD, GRP), lambda g: (g, 0, 0, 0))
    st = pl.BlockSpec((GRP, N_HEADS, HEAD, HEAD), lambda g: (g, 0, 0, 0))
    col = pl.BlockSpec((half, 1), lambda g: (0, 0))
    rb = row0 // GRP

    def rows(seg):
        return pl.BlockSpec((GRP, W_BR), lambda g: (rb + g, seg))

    return pl.pallas_call(
        _ret_sample_kernel,
        grid=(ns // GRP,),
        in_specs=[cm, cm, rows(SEG_VC), rows(SEG_GC), col, col,
                  pl.BlockSpec((N_HEADS, 1, HEAD), lambda g: (0, 0, 0)), st],
        out_specs=[pl.BlockSpec((GRP, W_BR), lambda g: (g, 0)), st],
        out_shape=[jax.ShapeDtypeStruct((ns, W_BR), BF16), jax.ShapeDtypeStruct(state.shape, F32)],
        scratch_shapes=[pltpu.VMEM((GRP, W_BR), F32)],
        compiler_params=_params(("arbitrary",), 48),
        name="ret_sample",
    )(_channel_major(zs, SEG_QC), _channel_major(zs, SEG_KC), z, z, cosc, sinc, gam, state)


TM = 1040
TM_DOWN = 520
TM_ROW = 208
TB_SEQ = 256


def kernel(x_prompt, x_sample, state_hgrn, state_lru_h, state_lru_conv, state_ret, w_in, hgrn_lb_logits, lru_conv_w, lru_conv_b, lru_wa, lru_ba, lru_wx, lru_bx, lru_lambda, w_branch, w_out, ln1_g, ln1_b, ln2_g, ln2_b, ffn_w1, ffn_w3, ffn_w2, router, moe_w1, moe_w3, moe_w2):
    nb, T, _ = x_prompt.shape
    ns = x_sample.shape[0]
    mp = nb * T
    lru_p = dict(lru_conv_w=lru_conv_w, lru_conv_b=lru_conv_b, lru_wa=lru_wa, lru_ba=lru_ba,
                 lru_wx=lru_wx, lru_bx=lru_bx, lru_lambda=lru_lambda)

    lb_all = hgrn_lower_bounds(hgrn_lb_logits)
    x = jnp.concatenate([x_prompt.reshape(mp, D_MODEL), x_sample.reshape(ns, D_MODEL)], axis=0)
    xb = to_bf16(x, TM_ROW)
    router_t = router.transpose(0, 2, 1)

    new_p, new_s = [], []
    for l in range(DEPTH):
        z = matmul(xb, w_in, l, TM, 512, name="in_proj")
        zs = z[mp:]
        lb = lb_all[l:l + 1]

        ya_p, hgrn_p = hgrn_prompt(z, lb, nb, T, TB_SEQ)
        yb_p, lru_h_p, conv_p = lru_prompt(z, lru_p, l, nb, T, TB_SEQ)
        yc_p, ret_p = ret_prompt(z, nb, T, TB_SEQ)

        ya_s, hgrn_s = hgrn_sample(z, zs, lb, state_hgrn[l], mp)
        yb_s, lru_h_s, conv_s = lru_sample(z, lru_p, l, mp // ns, ns, state_lru_conv[l].transpose(1, 0, 2),
                                           state_lru_h[l])
        yc_s, ret_s = ret_sample(z, zs, state_ret[l], mp)
        new_p.append((hgrn_p, lru_h_p, conv_p, ret_p))
        new_s.append((hgrn_s, lru_h_s, conv_s.transpose(1, 0, 2), ret_s))

        ya = jnp.concatenate([ya_p, ya_s], axis=0)
        yb = jnp.concatenate([yb_p, yb_s], axis=0)
        yc = jnp.concatenate([yc_p, yc_s], axis=0)
        merged = branch_merge(ya, yb, yc, w_branch, l, z, TM, 512)
        mix = matmul(merged, w_out, l, TM, 512, name="out_proj")
        x, xb = residual_ln(x, mix, ln1_g, ln1_b, l, TM_ROW)

        if l % 2 == 0:
            h = swiglu_up(xb, ffn_w1, ffn_w3, (l // 2,), TM, 256, name="ffn_up")
            y = down_proj(h, ffn_w2, (l // 2,), 0, TM_DOWN, 512, name="ffn_down")
            y = down_proj(h, ffn_w2, (l // 2,), 1, TM_DOWN, 512, prev=y, name="ffn_down")
        else:
            gate = router_gates(x, router_t, l // 2, TM_DOWN)
            y = None
            for e in range(N_EXPERTS):
                h = swiglu_up(xb, moe_w1, moe_w3, (l // 2, e), TM, 256, name="moe_up")
                y = down_proj(h, moe_w2, (l // 2, e), None, TM_DOWN, 512, prev=y, gate=gate, expert=e,
                              name="moe_down")
        x, xb = residual_ln(x, y, ln2_g, ln2_b, l, TM_ROW)

    def stack(states, j):
        return jnp.stack([s[j] for s in states], axis=0)

    return (x[:mp].reshape(nb, T, D_MODEL), x[mp:].reshape(ns, 1, D_MODEL),
            stack(new_p, 0), stack(new_p, 1), stack(new_p, 2), stack(new_p, 3),
            stack(new_s, 0), stack(new_s, 1), stack(new_s, 2), stack(new_s, 3))
```

```python
import functools

import jax
import jax.numpy as jnp
from jax import lax
from jax.experimental import pallas as pl
from jax.experimental.pallas import tpu as pltpu

F32 = jnp.float32
BF16 = jnp.bfloat16

LANE = 128
SUBLANE = 8
MIB = 1024 * 1024

D_MODEL = 4096
DEPTH = 4
N_HEADS = 8
HEAD = 128
W_BR = 1024
CONV_W = 4
LRU_C = 8.0
N_BRANCH = 3
N_IN = 10 * W_BR + N_BRANCH * D_MODEL
N_EXPERTS = 8
LN_EPS = 1e-5
RMS_EPS = 1e-6
ALPHA = (2 * DEPTH) ** 0.25
ROPE_BASE = 10000.0
PAST_LEN = 16384

SEG_QA, SEG_FA, SEG_IA, SEG_GA, SEG_XB, SEG_GB, SEG_QC, SEG_KC, SEG_VC, SEG_GC = range(10)
GATE_COL0 = 10 * W_BR

GLA_CHUNK = 64
GLA_SUB = 16
NEG_BIG = -1e30


def _params(sem, vmem_mib):
    return pltpu.CompilerParams(dimension_semantics=sem, vmem_limit_bytes=vmem_mib * MIB)


def _dot(a, b):
    return jnp.dot(a, b, preferred_element_type=F32)


def _dot_nt(a, b):
    return lax.dot_general(a, b, (((1,), (1,)), ((), ())), preferred_element_type=F32)


def _dot_tn(a, b):
    return lax.dot_general(a, b, (((0,), (0,)), ((), ())), preferred_element_type=F32)


def _sigmoid(x):
    return 1.0 / (1.0 + jnp.exp(-x))


def _silu(x):
    return x * _sigmoid(x)


def _gelu_tanh(x):
    return 0.5 * x * (1.0 + jnp.tanh(0.7978845608028654 * (x + 0.044715 * (x * x * x))))


def _head_rms(o):
    return o * lax.rsqrt(jnp.mean(o * o, axis=-1, keepdims=True) + RMS_EPS)


def _lb_kernel(g_ref, o_ref):
    g = g_ref[...]
    rows = [g[j:j + 1, :] for j in range(DEPTH)]
    m = rows[0]
    for r in rows[1:]:
        m = jnp.maximum(m, r)
    e = [jnp.exp(r - m) for r in rows]
    tot = e[0]
    for x in e[1:]:
        tot = tot + x
    acc = jnp.zeros_like(m)
    for j in range(DEPTH):
        o_ref[j:j + 1, :] = acc
        acc = acc + e[j] / tot


def hgrn_lower_bounds(logits):
    return pl.pallas_call(_lb_kernel, out_shape=jax.ShapeDtypeStruct(logits.shape, F32), name="hgrn_lb")(logits)


def _mm_kernel(x_ref, w_ref, o_ref, wb_ref):
    @pl.when(pl.program_id(1) == 0)
    def _():
        wb_ref[...] = w_ref[...].astype(BF16)

    o_ref[...] = _dot(x_ref[...], wb_ref[...]).astype(o_ref.dtype)


def matmul(x, w_stack, layer, tm, tn, out_dtype=F32, name="matmul"):
    M, K = x.shape
    N = w_stack.shape[-1]
    assert M % tm == 0 and N % tn == 0
    return pl.pallas_call(
        _mm_kernel,
        grid=(N // tn, M // tm),
        in_specs=[pl.BlockSpec((tm, K), lambda j, i: (i, 0)),
                  pl.BlockSpec((None, K, tn), lambda j, i: (layer, 0, j))],
        out_specs=pl.BlockSpec((tm, tn), lambda j, i: (i, j)),
        out_shape=jax.ShapeDtypeStruct((M, N), out_dtype),
        scratch_shapes=[pltpu.VMEM((K, tn), BF16)],
        compiler_params=_params(("arbitrary", "arbitrary"), 56),
        name=name,
    )(x, w_stack)


def _swiglu_up_kernel(x_ref, w1_ref, w3_ref, o_ref, wb_ref, *, tn):
    @pl.when(pl.program_id(1) == 0)
    def _():
        wb_ref[:, :tn] = w1_ref[...].astype(BF16)
        wb_ref[:, tn:] = w3_ref[...].astype(BF16)

    ab = _dot(x_ref[...], wb_ref[...])
    o_ref[...] = (_silu(ab[:, :tn]) * ab[:, tn:]).astype(o_ref.dtype)


def swiglu_up(x, w1, w3, idx, tm, tn, name="swiglu_up"):
    M, K = x.shape
    FF = w1.shape[-1]
    nlead = len(idx)
    wspec = pl.BlockSpec((None,) * nlead + (K, tn), lambda j, i: idx + (0, j))
    return pl.pallas_call(
        functools.partial(_swiglu_up_kernel, tn=tn),
        grid=(pl.cdiv(FF, tn), M // tm),
        in_specs=[pl.BlockSpec((tm, K), lambda j, i: (i, 0)), wspec, wspec],
        out_specs=pl.BlockSpec((tm, tn), lambda j, i: (i, j)),
        out_shape=jax.ShapeDtypeStruct((M, FF), BF16),
        scratch_shapes=[pltpu.VMEM((K, 2 * tn), BF16)],
        compiler_params=_params(("arbitrary", "arbitrary"), 56),
        name=name,
    )(x, w1, w3)


def _down_kernel(h_ref, w_ref, o_ref, wb_ref):
    @pl.when(pl.program_id(1) == 0)
    def _():
        wb_ref[...] = w_ref[...].astype(BF16)

    o_ref[...] = _dot(h_ref[...], wb_ref[...])


def _down_acc_kernel(h_ref, w_ref, p_ref, o_ref, wb_ref):
    @pl.when(pl.program_id(1) == 0)
    def _():
        wb_ref[...] = w_ref[...].astype(BF16)

    o_ref[...] = p_ref[...] + _dot(h_ref[...], wb_ref[...])


K_HALVES = 2


def down_proj(h, w, layer, kb, tm, tn, prev=None, name="down"):
    M = h.shape[0]
    Kb = h.shape[1] // K_HALVES
    N = w.shape[-1]
    h_spec = pl.BlockSpec((tm, Kb), lambda j, i: (i, kb))
    w_spec = pl.BlockSpec((None, Kb, tn), lambda j, i: (layer, kb, j))
    o_spec = pl.BlockSpec((tm, tn), lambda j, i: (i, j))
    args, specs, alias = [h, w], [h_spec, w_spec], {}
    if prev is not None:
        body = _down_acc_kernel
        args.append(prev)
        specs.append(o_spec)
        alias = {2: 0}
    else:
        body = _down_kernel
    return pl.pallas_call(
        body,
        grid=(N // tn, M // tm),
        in_specs=specs,
        out_specs=o_spec,
        out_shape=jax.ShapeDtypeStruct((M, N), F32),
        scratch_shapes=[pltpu.VMEM((Kb, tn), BF16)],
        input_output_aliases=alias,
        compiler_params=_params(("arbitrary", "arbitrary"), 56),
        name=name,
    )(*args)


def _merge_kernel(ya_ref, yb_ref, yc_ref, w_ref, ga_ref, gb_ref, gc_ref, o_ref, wb_ref):
    @pl.when(pl.program_id(1) == 0)
    def _():
        wb_ref[...] = w_ref[...].astype(BF16)

    acc = _sigmoid(ga_ref[...]) * _dot(ya_ref[...], wb_ref[0])
    acc = acc + _sigmoid(gb_ref[...]) * _dot(yb_ref[...], wb_ref[1])
    acc = acc + _sigmoid(gc_ref[...]) * _dot(yc_ref[...], wb_ref[2])
    o_ref[...] = acc.astype(o_ref.dtype)


def branch_merge(ya, yb, yc, w_branch, layer, z, tm, tn):
    M = ya.shape[0]
    nt = D_MODEL // tn
    g0 = GATE_COL0 // tn
    y_spec = pl.BlockSpec((tm, W_BR), lambda j, i: (i, 0))

    def gate_spec(b):
        return pl.BlockSpec((tm, tn), lambda j, i: (i, g0 + b * nt + j))

    return pl.pallas_call(
        _merge_kernel,
        grid=(nt, M // tm),
        in_specs=[y_spec, y_spec, y_spec,
                  pl.BlockSpec((None, N_BRANCH, W_BR, tn), lambda j, i: (layer, 0, 0, j)),
                  gate_spec(0), gate_spec(1), gate_spec(2)],
        out_specs=pl.BlockSpec((tm, tn), lambda j, i: (i, j)),
        out_shape=jax.ShapeDtypeStruct((M, D_MODEL), BF16),
        scratch_shapes=[pltpu.VMEM((N_BRANCH, W_BR, tn), BF16)],
        compiler_params=_params(("arbitrary", "arbitrary"), 48),
        name="branch_merge",
    )(ya, yb, yc, w_branch, z, z, z)


def _ln_store(v, g_ref, b_ref, o_ref, ob_ref):
    mu = jnp.mean(v, axis=-1, keepdims=True)
    c = v - mu
    var = jnp.mean(c * c, axis=-1, keepdims=True)
    o = c * lax.rsqrt(var + LN_EPS) * g_ref[...] + b_ref[...]
    o_ref[...] = o
    ob_ref[...] = o.astype(BF16)


def _ln_kernel(x_ref, y_ref, g_ref, b_ref, o_ref, ob_ref):
    _ln_store(ALPHA * x_ref[...] + y_ref[...], g_ref, b_ref, o_ref, ob_ref)


def residual_ln(x, y, g, b, layer, tm):
    M = x.shape[0]
    row = pl.BlockSpec((tm, D_MODEL), lambda i: (i, 0))
    vec = pl.BlockSpec((None, 1, D_MODEL), lambda i: (layer, 0, 0))
    return pl.pallas_call(
        _ln_kernel,
        grid=(M // tm,),
        in_specs=[row, row, vec, vec],
        out_specs=[row, row],
        out_shape=[jax.ShapeDtypeStruct((M, D_MODEL), F32), jax.ShapeDtypeStruct((M, D_MODEL), BF16)],
        compiler_params=_params(("arbitrary",), 48),
        name="residual_ln",
    )(x, y, g.reshape(DEPTH, 1, D_MODEL), b.reshape(DEPTH, 1, D_MODEL))


def _cast_kernel(x_ref, o_ref):
    o_ref[...] = x_ref[...].astype(BF16)


def to_bf16(x, tm):
    M, N = x.shape
    spec = pl.BlockSpec((tm, N), lambda i: (i, 0))
    return pl.pallas_call(_cast_kernel, grid=(M // tm,), in_specs=[spec], out_specs=spec,
                          out_shape=jax.ShapeDtypeStruct((M, N), BF16),
                          compiler_params=_params(("arbitrary",), 32), name="to_bf16")(x)


TOP_K = 2
MOE_TM = 512


def _top1(cols):
    m = cols[0]
    for c in cols[1:]:
        m = jnp.maximum(m, c)
    idx = jnp.full(m.shape, N_EXPERTS, jnp.int32)
    for e in reversed(range(N_EXPERTS)):
        idx = jnp.where(cols[e] == m, e, idx)
    return m, idx


def _router_kernel(x_ref, rt_ref, idx_ref, w_ref):
    x = x_ref[...]
    cols = [jnp.sum(x * rt_ref[e:e + 1, :], axis=-1, keepdims=True) for e in range(N_EXPERTS)]
    m1, i1 = _top1(cols)
    m2, i2 = _top1([jnp.where(i1 == e, -jnp.inf, cols[e]) for e in range(N_EXPERTS)])
    e2 = jnp.exp(m2 - m1)
    den = 1.0 + e2
    first = lax.broadcasted_iota(jnp.int32, idx_ref.shape, 1) == 0
    idx_ref[...] = jnp.where(first, i1, i2)
    w_ref[...] = jnp.where(first, 1.0 / den, e2 / den)


def router_top2(x, router_t, idx, tm):
    M = x.shape[0]
    out = pl.BlockSpec((tm, TOP_K), lambda i: (i, 0))
    return pl.pallas_call(
        _router_kernel,
        grid=(M // tm,),
        in_specs=[pl.BlockSpec((tm, D_MODEL), lambda i: (i, 0)),
                  pl.BlockSpec((None, N_EXPERTS, D_MODEL), lambda i: (idx, 0, 0))],
        out_specs=[out, out],
        out_shape=[jax.ShapeDtypeStruct((M, TOP_K), jnp.int32), jax.ShapeDtypeStruct((M, TOP_K), F32)],
        compiler_params=_params(("arbitrary",), 48),
        name="router",
    )(x, router_t)


def moe_schedule(idx, wts):
    n_pairs = idx.size
    nt = n_pairs // MOE_TM + N_EXPERTS
    n_slots = nt * MOE_TM
    e_flat = idx.reshape(-1)
    onehot = (e_flat[:, None] == jnp.arange(N_EXPERTS, dtype=jnp.int32)[None, :]).astype(jnp.int32)
    csum = jnp.cumsum(onehot, axis=0)
    rank = jnp.sum((csum - onehot) * onehot, axis=1)
    padded = (csum[-1] + (MOE_TM - 1)) // MOE_TM * MOE_TM
    gend = jnp.cumsum(padded)
    slot = (gend - padded)[e_flat] + rank
    src = jnp.zeros((n_slots,), jnp.int32).at[slot].set(jnp.arange(n_pairs, dtype=jnp.int32) // TOP_K)
    wslot = jnp.zeros((n_slots,), F32).at[slot].set(wts.reshape(-1)).reshape(n_slots, 1)
    n_used = gend[-1] // MOE_TM
    tstart = jnp.arange(nt, dtype=jnp.int32) * MOE_TM
    tstart = jnp.minimum(tstart, (n_used - 1) * MOE_TM)
    tile_e = jnp.sum((tstart[:, None] >= gend[None, :]).astype(jnp.int32), axis=1)
    return src, wslot, slot.astype(jnp.int32), tile_e.astype(jnp.int32), n_used.reshape(1).astype(jnp.int32)


def _row_copy(src_hbm, row, dst, i, sem):
    return pltpu.make_async_copy(src_hbm.at[pl.ds(row, 1)], dst.at[pl.ds(i, 1)], sem)


def _gather_cast_kernel(src_ref, nu_ref, x_hbm, o_ref, buf, sem):
    t = pl.program_id(0)
    base = t * MOE_TM

    @pl.when(t < nu_ref[0])
    def _():
        def issue(i, c):
            _row_copy(x_hbm, src_ref[base + i], buf, i, sem).start()
            return c

        def drain(i, c):
            _row_copy(x_hbm, 0, buf, i, sem).wait()
            return c

        lax.fori_loop(0, MOE_TM, issue, 0)
        lax.fori_loop(0, MOE_TM, drain, 0)
        o_ref[...] = buf[...].astype(BF16)

    @pl.when(t >= nu_ref[0])
    def _():
        o_ref[...] = jnp.zeros_like(o_ref)


def moe_gather(x, src, n_used):
    n_slots = src.shape[0]
    return pl.pallas_call(
        _gather_cast_kernel,
        grid_spec=pltpu.PrefetchScalarGridSpec(
            num_scalar_prefetch=2,
            grid=(n_slots // MOE_TM,),
            in_specs=[pl.BlockSpec(memory_space=pl.ANY)],
            out_specs=pl.BlockSpec((MOE_TM, D_MODEL), lambda t, src, nu: (t, 0)),
            scratch_shapes=[pltpu.VMEM((MOE_TM, D_MODEL), F32), pltpu.SemaphoreType.DMA(())]),
        out_shape=jax.ShapeDtypeStruct((n_slots, D_MODEL), BF16),
        compiler_params=_params(("arbitrary",), 32),
        name="moe_gather",
    )(src, n_used, x)


def _new_expert(te_ref, t):
    return (t == 0) | (te_ref[t] != te_ref[jnp.maximum(t - 1, 0)])


def _moe_up_kernel(te_ref, nu_ref, x_ref, w1_ref, w3_ref, o_ref, wb_ref, *, tn):
    t = pl.program_id(1)

    @pl.when(_new_expert(te_ref, t))
    def _():
        wb_ref[:, :tn] = w1_ref[...].astype(BF16)
        wb_ref[:, tn:] = w3_ref[...].astype(BF16)

    @pl.when(t < nu_ref[0])
    def _():
        ab = _dot(x_ref[...], wb_ref[...])
        o_ref[...] = (_silu(ab[:, :tn]) * ab[:, tn:]).astype(o_ref.dtype)

    @pl.when(t >= nu_ref[0])
    def _():
        o_ref[...] = jnp.zeros_like(o_ref)


def moe_up(xs, w1, w3, layer, tile_e, n_used, tn):
    n_slots, K = xs.shape
    FF = w1.shape[-1]
    wspec = pl.BlockSpec((None, None, K, tn), lambda j, t, te, nu: (layer, te[t], 0, j))
    return pl.pallas_call(
        functools.partial(_moe_up_kernel, tn=tn),
        grid_spec=pltpu.PrefetchScalarGridSpec(
            num_scalar_prefetch=2,
            grid=(pl.cdiv(FF, tn), n_slots // MOE_TM),
            in_specs=[pl.BlockSpec((MOE_TM, K), lambda j, t, te, nu: (t, 0)), wspec, wspec],
            out_specs=pl.BlockSpec((MOE_TM, tn), lambda j, t, te, nu: (t, j)),
            scratch_shapes=[pltpu.VMEM((K, 2 * tn), BF16)]),
        out_shape=jax.ShapeDtypeStruct((n_slots, FF), BF16),
        compiler_params=_params(("arbitrary", "arbitrary"), 56),
        name="moe_up",
    )(tile_e, n_used, xs, w1, w3)


def _moe_down_kernel(te_ref, nu_ref, h_ref, w_ref, g_ref, o_ref, wb_ref):
    t = pl.program_id(1)

    @pl.when(_new_expert(te_ref, t))
    def _():
        wb_ref[...] = w_ref[...].astype(BF16)

    @pl.when(t < nu_ref[0])
    def _():
        o_ref[...] = g_ref[...] * _dot(h_ref[...], wb_ref[...])

    @pl.when(t >= nu_ref[0])
    def _():
        o_ref[...] = jnp.zeros_like(o_ref)


def moe_down(hs, w2, layer, wslot, tile_e, n_used, tn):
    n_slots, K = hs.shape
    N = w2.shape[-1]
    return pl.pallas_call(
        _moe_down_kernel,
        grid_spec=pltpu.PrefetchScalarGridSpec(
            num_scalar_prefetch=2,
            grid=(N // tn, n_slots // MOE_TM),
            in_specs=[pl.BlockSpec((MOE_TM, K), lambda j, t, te, nu: (t, 0)),
                      pl.BlockSpec((None, None, K, tn), lambda j, t, te, nu: (layer, te[t], 0, j)),
                      pl.BlockSpec((MOE_TM, 1), lambda j, t, te, nu: (t, 0))],
            out_specs=pl.BlockSpec((MOE_TM, tn), lambda j, t, te, nu: (t, j)),
            scratch_shapes=[pltpu.VMEM((K, tn), BF16)]),
        out_shape=jax.ShapeDtypeStruct((n_slots, N), F32),
        compiler_params=_params(("arbitrary", "arbitrary"), 56),
        name="moe_down",
    )(tile_e, n_used, hs, w2, wslot)


def _ln_moe_kernel(slot_ref, x_ref, eo_hbm, g_ref, b_ref, o_ref, ob_ref, ya, yb, sem, *, tm):
    base = pl.program_id(0) * tm

    def issue(i, c):
        p = TOP_K * (base + i)
        _row_copy(eo_hbm, slot_ref[p], ya, i, sem).start()
        _row_copy(eo_hbm, slot_ref[p + 1], yb, i, sem).start()
        return c

    def drain(i, c):
        _row_copy(eo_hbm, 0, ya, i, sem).wait()
        _row_copy(eo_hbm, 0, yb, i, sem).wait()
        return c

    lax.fori_loop(0, tm, issue, 0)
    lax.fori_loop(0, tm, drain, 0)
    _ln_store(ALPHA * x_ref[...] + (ya[...] + yb[...]), g_ref, b_ref, o_ref, ob_ref)


def residual_ln_moe(x, eo, slot, g, b, layer, tm):
    M = x.shape[0]
    row = pl.BlockSpec((tm, D_MODEL), lambda i, s: (i, 0))
    vec = pl.BlockSpec((None, 1, D_MODEL), lambda i, s: (layer, 0, 0))
    return pl.pallas_call(
        functools.partial(_ln_moe_kernel, tm=tm),
        grid_spec=pltpu.PrefetchScalarGridSpec(
            num_scalar_prefetch=1,
            grid=(M // tm,),
            in_specs=[row, pl.BlockSpec(memory_space=pl.ANY), vec, vec],
            out_specs=[row, row],
            scratch_shapes=[pltpu.VMEM((tm, D_MODEL), F32), pltpu.VMEM((tm, D_MODEL), F32),
                            pltpu.SemaphoreType.DMA(())]),
        out_shape=[jax.ShapeDtypeStruct((M, D_MODEL), F32), jax.ShapeDtypeStruct((M, D_MODEL), BF16)],
        compiler_params=_params(("arbitrary",), 48),
        name="residual_ln_moe",
    )(slot, x, eo, g.reshape(DEPTH, 1, D_MODEL), b.reshape(DEPTH, 1, D_MODEL))


def _hgrn_same_subblock(q, k, v, b):
    SB = GLA_SUB
    half = SB // 2
    rid = lax.broadcasted_iota(jnp.int32, (SB, HEAD), 0)
    rid_hi = lax.broadcasted_iota(jnp.int32, (half, HEAD), 0) + half
    outs = []
    for i in range(q.shape[0] // SB):
        lo = i * SB
        qi, bi, ki, vi = q[lo:lo + SB], b[lo:lo + SB], k[lo:lo + SB], v[lo:lo + SB]
        acc = jnp.zeros((SB, HEAD), F32)
        acc_hi = jnp.zeros((half, HEAD), F32)
        for s in range(half):
            d = jnp.where(rid >= s, bi - bi[s:s + 1], NEG_BIG)
            w = qi * jnp.exp2(d) * ki[s:s + 1]
            acc = acc + jnp.sum(w, axis=1, keepdims=True) * vi[s:s + 1]
        qh, bh = qi[half:], bi[half:]
        for s in range(half, SB):
            d = jnp.where(rid_hi >= s, bh - bi[s:s + 1], NEG_BIG)
            w = qh * jnp.exp2(d) * ki[s:s + 1]
            acc_hi = acc_hi + jnp.sum(w, axis=1, keepdims=True) * vi[s:s + 1]
        outs.append(acc[:half])
        outs.append(acc[half:] + acc_hi)
    return jnp.concatenate(outs, axis=0)


def _hgrn_prompt_kernel(q_ref, f_ref, i_ref, g_ref, lb_ref, y_ref, s_ref, st_ref, *, tb, hp):
    t = pl.program_id(2)

    @pl.when(t == 0)
    def _():
        st_ref[...] = jnp.zeros_like(st_ref)

    C, SB = GLA_CHUNK, GLA_SUB
    nc = tb // C
    lb = lb_ref[...]
    f = lb + (1.0 - lb) * _sigmoid(f_ref[...])
    r = lax.broadcasted_iota(jnp.int32, (tb, tb), 0)
    c = lax.broadcasted_iota(jnp.int32, (tb, tb), 1)
    tril = ((r >= c) & (r // C == c // C)).astype(F32)
    b_all = jnp.dot(tril, jnp.log2(f), preferred_element_type=F32, precision=lax.Precision.HIGHEST)
    k_all = 1.0 - f
    same_chunk = r // C == c // C
    rid = lax.broadcasted_iota(jnp.int32, (tb, HEAD), 0)
    sub = (rid % C) // SB
    chunk = rid // C

    def row_bcast(x, row_of_chunk):
        return jnp.concatenate([jnp.broadcast_to(x[ci * C + row_of_chunk:ci * C + row_of_chunk + 1], (C, HEAD))
                                for ci in range(nc)], axis=0)

    for h in range(hp):
        cols = slice(h * HEAD, (h + 1) * HEAD)
        q, v = q_ref[:, cols], i_ref[:, cols]
        k, b = k_all[:, cols], b_all[:, cols]
        vb = v.astype(BF16)
        o = _hgrn_same_subblock(q, k, v, b)

        betas = [row_bcast(b, i * SB - 1) for i in range(1, C // SB)]
        beta_row = betas[-1]
        for i in range(C // SB - 2, 0, -1):
            beta_row = jnp.where(sub == i, betas[i - 1], beta_row)
        qd = q * jnp.exp2(jnp.where(sub > 0, b - beta_row, NEG_BIG))
        q_parts = [jnp.where(sub == i, qd, 0.0) for i in range(1, C // SB)]
        k_parts = [k * jnp.exp2(jnp.where(sub < i, betas[i - 1] - b, NEG_BIG)) for i in range(1, C // SB)]
        scores = _dot_nt(jnp.concatenate(q_parts, axis=1).astype(BF16), jnp.concatenate(k_parts, axis=1).astype(BF16))
        o = o + _dot(jnp.where(same_chunk, scores, 0.0).astype(BF16), vb)

        b_last = row_bcast(b, C - 1)
        kt = (k * jnp.exp2(b_last - b)).astype(BF16)
        zero = jnp.zeros_like(kt)
        upd = _dot_tn(vb, jnp.concatenate([jnp.where(chunk == ci, kt, zero) for ci in range(nc)], axis=1))
        st = st_ref[h]
        starts = []
        for ci in range(nc):
            starts.append(st.astype(BF16))
            st = jnp.exp2(b[ci * C + C - 1:ci * C + C]) * st + upd[:, ci * HEAD:(ci + 1) * HEAD]
        st_ref[h] = st
        inter = _dot_nt((q * jnp.exp2(b)).astype(BF16), jnp.concatenate(starts, axis=0))
        o = o + jnp.concatenate([inter[ci * C:(ci + 1) * C, ci * HEAD:(ci + 1) * HEAD] for ci in range(nc)], axis=0)
        y_ref[:, cols] = (_head_rms(o) * _silu(g_ref[:, cols])).astype(BF16)

    @pl.when(t == pl.num_programs(2) - 1)
    def _():
        for h in range(hp):
            s_ref[h] = st_ref[h].T


def hgrn_prompt(z, lb, nb, T, tb, hp, m_total):
    nt = T // tb
    ng = N_HEADS // hp

    def seg(s):
        return pl.BlockSpec((tb, hp * HEAD), lambda b, h, t: (b * nt + t, s * ng + h))

    return pl.pallas_call(
        functools.partial(_hgrn_prompt_kernel, tb=tb, hp=hp),
        grid=(nb, ng, nt),
        in_specs=[seg(SEG_QA), seg(SEG_FA), seg(SEG_IA), seg(SEG_GA),
                  pl.BlockSpec((1, hp * HEAD), lambda b, h, t: (0, h))],
        out_specs=[pl.BlockSpec((tb, hp * HEAD), lambda b, h, t: (b * nt + t, h)),
                   pl.BlockSpec((None, hp, HEAD, HEAD), lambda b, h, t: (b, h, 0, 0))],
        out_shape=[jax.ShapeDtypeStruct((m_total, W_BR), BF16),
                   jax.ShapeDtypeStruct((nb, N_HEADS, HEAD, HEAD), F32)],
        scratch_shapes=[pltpu.VMEM((hp, HEAD, HEAD), F32)],
        compiler_params=_params(("arbitrary", "arbitrary", "arbitrary"), 32),
        name="hgrn_prompt",
    )(z, z, z, z, lb)


def _rotate(x, cosf, sinf):
    return x * cosf + pltpu.roll(x, HEAD // 2, 1) * sinf


def _ret_prompt_kernel(q_ref, k_ref, v_ref, g_ref, cos_ref, sin_ref, lg_ref, y_ref, s_ref, st_ref, *, tb, hp):
    t = pl.program_id(2)

    @pl.when(t == 0)
    def _():
        st_ref[...] = jnp.zeros_like(st_ref)

    cosf, sinf = cos_ref[...], sin_ref[...]
    tpos = lax.broadcasted_iota(jnp.int32, (tb, HEAD), 0).astype(F32)
    r = lax.broadcasted_iota(jnp.int32, (tb, tb), 0)
    c = lax.broadcasted_iota(jnp.int32, (tb, tb), 1)
    causal = r >= c
    lag = (r - c).astype(F32)
    for h in range(hp):
        cols = slice(h * HEAD, (h + 1) * HEAD)
        lg_w = lg_ref[h]
        lg = lg_w[:, :HEAD]
        qr = _rotate(q_ref[:, cols], cosf, sinf)
        kr = _rotate(k_ref[:, cols], cosf, sinf) * (HEAD ** -0.5)
        vb = v_ref[:, cols].astype(BF16)
        dmat = jnp.where(causal, jnp.exp(lag * lg_w), 0.0)
        st = st_ref[h]
        scores = _dot_nt(qr.astype(BF16), kr.astype(BF16)) * dmat
        o = _dot(scores.astype(BF16), vb)
        o = o + _dot_nt((qr * jnp.exp((tpos + 1.0) * lg)).astype(BF16), st.astype(BF16))
        kt = (kr * jnp.exp((tb - 1.0 - tpos) * lg)).astype(BF16)
        st_ref[h] = jnp.exp(float(tb) * lg) * st + _dot_tn(vb, kt)
        y_ref[:, cols] = (_head_rms(o) * _silu(g_ref[:, cols])).astype(BF16)

    @pl.when(t == pl.num_programs(2) - 1)
    def _():
        for h in range(hp):
            s_ref[h] = st_ref[h].T


def _rope_tables(pos):
    half = HEAD // 2
    inv_freq = 1.0 / (ROPE_BASE ** jnp.linspace(0.0, 1.0, half, dtype=F32))
    ang = pos.astype(F32)[:, None] * inv_freq[None, :]
    cos, sin = jnp.cos(ang), jnp.sin(ang)
    return jnp.concatenate([cos, cos], axis=-1), jnp.concatenate([-sin, sin], axis=-1)


def _log_gamma():
    return jnp.log1p(-jnp.power(2.0, -5.0 - jnp.arange(N_HEADS, dtype=F32)))


def ret_prompt(z, nb, T, tb, hp, m_total):
    nt = T // tb
    ng = N_HEADS // hp
    cosf, sinf = _rope_tables(jnp.arange(T))
    lg = jnp.broadcast_to(_log_gamma()[:, None, None], (N_HEADS, 1, tb))

    def seg(s):
        return pl.BlockSpec((tb, hp * HEAD), lambda b, h, t: (b * nt + t, s * ng + h))

    tab = pl.BlockSpec((tb, HEAD), lambda b, h, t: (t, 0))
    return pl.pallas_call(
        functools.partial(_ret_prompt_kernel, tb=tb, hp=hp),
        grid=(nb, ng, nt),
        in_specs=[seg(SEG_QC), seg(SEG_KC), seg(SEG_VC), seg(SEG_GC), tab, tab,
                  pl.BlockSpec((hp, 1, tb), lambda b, h, t: (h, 0, 0))],
        out_specs=[pl.BlockSpec((tb, hp * HEAD), lambda b, h, t: (b * nt + t, h)),
                   pl.BlockSpec((None, hp, HEAD, HEAD), lambda b, h, t: (b, h, 0, 0))],
        out_shape=[jax.ShapeDtypeStruct((m_total, W_BR), BF16),
                   jax.ShapeDtypeStruct((nb, N_HEADS, HEAD, HEAD), F32)],
        scratch_shapes=[pltpu.VMEM((hp, HEAD, HEAD), F32)],
        compiler_params=_params(("arbitrary", "arbitrary", "arbitrary"), 32),
        name="ret_prompt",
    )(z, z, z, z, cosf, sinf, lg)


def _lru_gates(xc, wa_ref, ba_ref, wx_ref, bx_ref, lam_ref):
    xcb = xc.astype(BF16)
    ra, ia = [], []
    for n in range(W_BR // HEAD):
        xn = xcb[:, n * HEAD:(n + 1) * HEAD]
        ra.append(_dot(xn, wa_ref[n].astype(BF16)))
        ia.append(_dot(xn, wx_ref[n].astype(BF16)))
    r = _sigmoid(jnp.concatenate(ra, axis=1) + ba_ref[...])
    i = _sigmoid(jnp.concatenate(ia, axis=1) + bx_ref[...])
    nlam = -lam_ref[...]
    softplus = jnp.maximum(nlam, 0.0) + jnp.log1p(jnp.exp(-jnp.abs(nlam)))
    log_a = -LRU_C * r * softplus
    a = jnp.exp(log_a)
    th = jnp.tanh(log_a)
    u = jnp.sqrt(-2.0 * th / (1.0 - th)) * (i * xc)
    return a, u


def _lru_prompt_kernel(x_ref, g_ref, cw_ref, cb_ref, wa_ref, ba_ref, wx_ref, bx_ref, lam_ref,
                       y_ref, h_ref, conv_ref, prev_ref, hc_ref, *, tb):
    t = pl.program_id(1)

    @pl.when(t == 0)
    def _():
        prev_ref[...] = jnp.zeros_like(prev_ref)
        hc_ref[...] = jnp.zeros_like(hc_ref)

    x = x_ref[...]
    prev = prev_ref[...]
    rid = lax.broadcasted_iota(jnp.int32, (tb, W_BR), 0)
    rid8 = lax.broadcasted_iota(jnp.int32, (SUBLANE, W_BR), 0)
    cw = cw_ref[...]
    xc = cb_ref[...] + cw[CONV_W - 1:CONV_W] * x
    for j in range(1, CONV_W):
        xr = pltpu.roll(x, j, 0)
        head = jnp.where(rid8 < j, pltpu.roll(prev, j, 0), xr[:SUBLANE])
        xc = xc + cw[CONV_W - 1 - j:CONV_W - j] * jnp.concatenate([head, xr[SUBLANE:]], axis=0)
    a, u = _lru_gates(xc, wa_ref, ba_ref, wx_ref, bx_ref, lam_ref)
    d = 1
    while d < tb:
        keep = rid >= d
        a_s = jnp.where(keep, pltpu.roll(a, d, 0), 1.0)
        u_s = jnp.where(keep, pltpu.roll(u, d, 0), 0.0)
        u = a * u_s + u
        a = a * a_s
        d *= 2
    h = a * hc_ref[0:1, :] + u
    y_ref[...] = (h * _gelu_tanh(g_ref[...])).astype(BF16)
    hc_ref[0:1, :] = h[tb - 1:tb]
    prev_ref[...] = x[tb - SUBLANE:tb]

    @pl.when(t == pl.num_programs(1) - 1)
    def _():
        h_ref[...] = h[tb - 1:tb]
        conv_ref[...] = x[tb - (CONV_W - 1):tb]


def _lru_weight_specs(layer):
    def vec():
        return pl.BlockSpec((None, 1, W_BR), lambda *_: (layer, 0, 0))

    def blk():
        return pl.BlockSpec((None, W_BR // HEAD, HEAD, HEAD), lambda *_: (layer, 0, 0, 0))

    return [pl.BlockSpec((None, CONV_W, W_BR), lambda *_: (layer, 0, 0)), vec(), blk(), vec(), blk(), vec(), vec()]


def _lru_weight_args(p):
    v = lambda a: a.reshape(DEPTH, 1, W_BR)
    return [p["lru_conv_w"], v(p["lru_conv_b"]), p["lru_wa"], v(p["lru_ba"]), p["lru_wx"], v(p["lru_bx"]),
            v(p["lru_lambda"])]


def lru_prompt(z, p, layer, nb, T, tb, m_total):
    nt = T // tb
    outs = pl.pallas_call(
        functools.partial(_lru_prompt_kernel, tb=tb),
        grid=(nb, nt),
        in_specs=[pl.BlockSpec((tb, W_BR), lambda b, t: (b * nt + t, SEG_XB)),
                  pl.BlockSpec((tb, W_BR), lambda b, t: (b * nt + t, SEG_GB))] + _lru_weight_specs(layer),
        out_specs=[pl.BlockSpec((tb, W_BR), lambda b, t: (b * nt + t, 0)),
                   pl.BlockSpec((None, 1, W_BR), lambda b, t: (b, 0, 0)),
                   pl.BlockSpec((None, CONV_W - 1, W_BR), lambda b, t: (b, 0, 0))],
        out_shape=[jax.ShapeDtypeStruct((m_total, W_BR), BF16),
                   jax.ShapeDtypeStruct((nb, 1, W_BR), F32),
                   jax.ShapeDtypeStruct((nb, CONV_W - 1, W_BR), F32)],
        scratch_shapes=[pltpu.VMEM((SUBLANE, W_BR), F32), pltpu.VMEM((SUBLANE, W_BR), F32)],
        compiler_params=_params(("arbitrary", "arbitrary"), 48),
        name="lru_prompt",
    )(z, z, *_lru_weight_args(p))
    return outs[0], outs[1].reshape(nb, W_BR), outs[2]


def _lru_sample_kernel(x_ref, g_ref, buf_ref, h0_ref, cw_ref, cb_ref, wa_ref, ba_ref, wx_ref, bx_ref, lam_ref,
                       y_in_ref, y_ref, h_ref, conv_ref):
    del y_in_ref
    x = x_ref[...]
    cw = cw_ref[...]
    xc = cb_ref[...] + cw[CONV_W - 1:CONV_W] * x
    for j in range(CONV_W - 1):
        xc = xc + cw[j:j + 1] * buf_ref[j]
    a, u = _lru_gates(xc, wa_ref, ba_ref, wx_ref, bx_ref, lam_ref)
    h = a * h0_ref[...] + u
    y_ref[...] = (h * _gelu_tanh(g_ref[...])).astype(BF16)
    h_ref[...] = h
    for j in range(CONV_W - 2):
        conv_ref[j] = buf_ref[j + 1]
    conv_ref[CONV_W - 2] = x


def lru_sample(z, p, layer, row_blk, ns, buf_t, h0, y):
    full3 = pl.BlockSpec((CONV_W - 1, ns, W_BR), lambda i: (0, 0, 0))
    full2 = pl.BlockSpec((ns, W_BR), lambda i: (0, 0))
    n_in = 4 + len(_lru_weight_specs(layer))
    return pl.pallas_call(
        _lru_sample_kernel,
        grid=(1,),
        in_specs=[pl.BlockSpec((ns, W_BR), lambda i: (row_blk, SEG_XB)),
                  pl.BlockSpec((ns, W_BR), lambda i: (row_blk, SEG_GB)), full3, full2] + _lru_weight_specs(layer)
                 + [pl.BlockSpec(memory_space=pl.ANY)],
        out_specs=[pl.BlockSpec((ns, W_BR), lambda i: (row_blk, 0)), full2, full3],
        out_shape=[jax.ShapeDtypeStruct(y.shape, BF16), jax.ShapeDtypeStruct((ns, W_BR), F32),
                   jax.ShapeDtypeStruct((CONV_W - 1, ns, W_BR), F32)],
        input_output_aliases={n_in: 0},
        compiler_params=_params(("arbitrary",), 32),
        name="lru_sample",
    )(z, z, buf_t, h0, *_lru_weight_args(p), y)


GRP = SUBLANE


def _channel_major(zs, seg):
    ns = zs.shape[0]
    x = zs[:, seg * W_BR:(seg + 1) * W_BR].reshape(ns // GRP, GRP, N_HEADS, HEAD)
    return x.transpose(0, 2, 3, 1)


def _hgrn_sample_kernel(q_ref, f_ref, lb_ref, v_ref, g_ref, s_ref, *aliased_and_outs):
    y_ref, so_ref, o_scr = aliased_and_outs[-3:]
    for h in range(N_HEADS):
        lb = lb_ref[h]
        f = lb + (1.0 - lb) * _sigmoid(f_ref[h])
        k = 1.0 - f
        q = q_ref[h]
        for j in range(GRP):
            vrow = v_ref[j:j + 1, h * HEAD:(h + 1) * HEAD]
            sn = f[:, j:j + 1] * s_ref[j, h] + k[:, j:j + 1] * vrow
            so_ref[j, h] = sn
            o_scr[j:j + 1, h * HEAD:(h + 1) * HEAD] = jnp.sum(q[:, j:j + 1] * sn, axis=0, keepdims=True)
    for h in range(N_HEADS):
        cols = slice(h * HEAD, (h + 1) * HEAD)
        y_ref[:, cols] = (_head_rms(o_scr[:, cols]) * _silu(g_ref[:, cols])).astype(BF16)


def _sample_state_call(body, name, lead_args, lead_specs, z, segs, states, layer, row0, y, stacked):
    ns = states.shape[1]
    rb = row0 // GRP
    st = pl.BlockSpec((None, GRP, N_HEADS, HEAD, HEAD), lambda g: (layer, g, 0, 0, 0))
    anyspec = pl.BlockSpec(memory_space=pl.ANY)
    in_specs = lead_specs + [pl.BlockSpec((GRP, W_BR), lambda g, s=s: (rb + g, s)) for s in segs] + [st, anyspec]
    args = lead_args + [z] * len(segs) + [states, y]
    alias = {len(args) - 1: 0}
    if stacked is not None:
        in_specs.append(anyspec)
        args.append(stacked)
        alias[len(args) - 1] = 1
    return pl.pallas_call(
        body,
        grid=(ns // GRP,),
        in_specs=in_specs,
        out_specs=[pl.BlockSpec((GRP, W_BR), lambda g: (rb + g, 0)), st],
        out_shape=[jax.ShapeDtypeStruct(y.shape, BF16), jax.ShapeDtypeStruct(states.shape, F32)],
        scratch_shapes=[pltpu.VMEM((GRP, W_BR), F32)],
        input_output_aliases=alias,
        compiler_params=_params(("arbitrary",), 48),
        name=name,
    )(*args)


def hgrn_sample(z, zs, lb, states, layer, row0, y, stacked):
    cm = pl.BlockSpec((None, N_HEADS, HEAD, GRP), lambda g: (g, 0, 0, 0))
    lead_specs = [cm, cm, pl.BlockSpec((N_HEADS, HEAD, 1), lambda g: (0, 0, 0))]
    lead_args = [_channel_major(zs, SEG_QA), _channel_major(zs, SEG_FA), lb.reshape(N_HEADS, HEAD, 1)]
    return _sample_state_call(_hgrn_sample_kernel, "hgrn_sample", lead_args, lead_specs, z, (SEG_IA, SEG_GA),
                              states, layer, row0, y, stacked)


def _ret_sample_kernel(q_ref, k_ref, cos_ref, sin_ref, gam_ref, v_ref, g_ref, s_ref, *aliased_and_outs):
    y_ref, so_ref, o_scr = aliased_and_outs[-3:]
    half = HEAD // 2
    cosc, sinc = cos_ref[...], sin_ref[...]

    def rot(x):
        x1, x2 = x[:half], x[half:]
        return jnp.concatenate([x1 * cosc - x2 * sinc, x1 * sinc + x2 * cosc], axis=0)

    for h in range(N_HEADS):
        q = rot(q_ref[h])
        k = rot(k_ref[h]) * (HEAD ** -0.5)
        gam = gam_ref[h]
        for j in range(GRP):
            vrow = v_ref[j:j + 1, h * HEAD:(h + 1) * HEAD]
            sn = gam * s_ref[j, h] + k[:, j:j + 1] * vrow
            so_ref[j, h] = sn
            o_scr[j:j + 1, h * HEAD:(h + 1) * HEAD] = jnp.sum(q[:, j:j + 1] * sn, axis=0, keepdims=True)
    for h in range(N_HEADS):
        cols = slice(h * HEAD, (h + 1) * HEAD)
        y_ref[:, cols] = (_head_rms(o_scr[:, cols]) * _silu(g_ref[:, cols])).astype(BF16)


def ret_sample(z, zs, states, layer, row0, y, stacked):
    half = HEAD // 2
    cosf, sinf = _rope_tables(jnp.full((1,), PAST_LEN))
    cosc = cosf[0, :half].reshape(half, 1)
    sinc = sinf[0, half:].reshape(half, 1)
    gam = jnp.broadcast_to(jnp.exp(_log_gamma())[:, None, None], (N_HEADS, 1, HEAD))
    cm = pl.BlockSpec((None, N_HEADS, HEAD, GRP), lambda g: (g, 0, 0, 0))
    col = pl.BlockSpec((half, 1), lambda g: (0, 0))
    lead_specs = [cm, cm, col, col, pl.BlockSpec((N_HEADS, 1, HEAD), lambda g: (0, 0, 0))]
    lead_args = [_channel_major(zs, SEG_QC), _channel_major(zs, SEG_KC), cosc, sinc, gam]
    return _sample_state_call(_ret_sample_kernel, "ret_sample", lead_args, lead_specs, z, (SEG_VC, SEG_GC),
                              states, layer, row0, y, stacked)


TM = 1040
TM_DOWN = 520
TM_ROW = 208
TB_SEQ = 256
HEADS_PER_STEP = 8


def kernel(x_prompt, x_sample, state_hgrn, state_lru_h, state_lru_conv, state_ret, w_in, hgrn_lb_logits, lru_conv_w, lru_conv_b, lru_wa, lru_ba, lru_wx, lru_bx, lru_lambda, w_branch, w_out, ln1_g, ln1_b, ln2_g, ln2_b, ffn_w1, ffn_w3, ffn_w2, router, moe_w1, moe_w3, moe_w2):
    nb, T, _ = x_prompt.shape
    ns = x_sample.shape[0]
    mp = nb * T
    lru_p = dict(lru_conv_w=lru_conv_w, lru_conv_b=lru_conv_b, lru_wa=lru_wa, lru_ba=lru_ba,
                 lru_wx=lru_wx, lru_bx=lru_bx, lru_lambda=lru_lambda)

    lb_all = hgrn_lower_bounds(hgrn_lb_logits)
    x = jnp.concatenate([x_prompt.reshape(mp, D_MODEL), x_sample.reshape(ns, D_MODEL)], axis=0)
    xb = to_bf16(x, TM_ROW)
    router_t = router.transpose(0, 2, 1)

    m_total = mp + ns
    new_p, new_s = [], []
    hgrn_s = ret_s = None
    for l in range(DEPTH):
        z = matmul(xb, w_in, l, TM, 512, name="in_proj")
        zs = z[mp:]
        lb = lb_all[l:l + 1]

        ya, hgrn_p = hgrn_prompt(z, lb, nb, T, TB_SEQ, HEADS_PER_STEP, m_total)
        yb, lru_h_p, conv_p = lru_prompt(z, lru_p, l, nb, T, TB_SEQ, m_total)
        yc, ret_p = ret_prompt(z, nb, T, TB_SEQ, HEADS_PER_STEP, m_total)

        ya, hgrn_s = hgrn_sample(z, zs, lb, state_hgrn, l, mp, ya, hgrn_s)
        yb, lru_h_s, conv_s = lru_sample(z, lru_p, l, mp // ns, ns, state_lru_conv[l].transpose(1, 0, 2),
                                         state_lru_h[l], yb)
        yc, ret_s = ret_sample(z, zs, state_ret, l, mp, yc, ret_s)
        new_p.append((hgrn_p, lru_h_p, conv_p, ret_p))
        new_s.append((lru_h_s, conv_s.transpose(1, 0, 2)))

        merged = branch_merge(ya, yb, yc, w_branch, l, z, TM, 512)
        mix = matmul(merged, w_out, l, TM, 512, name="out_proj")
        x, xb = residual_ln(x, mix, ln1_g, ln1_b, l, TM_ROW)

        if l % 2 == 0:
            h = swiglu_up(xb, ffn_w1, ffn_w3, (l // 2,), TM, 256, name="ffn_up")
            y = down_proj(h, ffn_w2, l // 2, 0, TM_DOWN, 512, name="ffn_down")
            y = down_proj(h, ffn_w2, l // 2, 1, TM_DOWN, 512, prev=y, name="ffn_down")
            x, xb = residual_ln(x, y, ln2_g, ln2_b, l, TM_ROW)
        else:
            eidx, ew = router_top2(x, router_t, l // 2, TM_DOWN)
            src, wslot, slot, tile_e, n_used = moe_schedule(eidx, ew)
            xs = moe_gather(x, src, n_used)
            hs = moe_up(xs, moe_w1, moe_w3, l // 2, tile_e, n_used, 512)
            eo = moe_down(hs, moe_w2, l // 2, wslot, tile_e, n_used, 512)
            x, xb = residual_ln_moe(x, eo, slot, ln2_g, ln2_b, l, TM_ROW)

    def stack(states, j):
        return jnp.stack([s[j] for s in states], axis=0)

    return (x[:mp].reshape(nb, T, D_MODEL), x[mp:].reshape(ns, 1, D_MODEL),
            stack(new_p, 0), stack(new_p, 1), stack(new_p, 2), stack(new_p, 3),
            hgrn_s, stack(new_s, 0), stack(new_s, 1), ret_s)
```

```python
import functools

import jax
import jax.numpy as jnp
from jax import lax
from jax.experimental import pallas as pl
from jax.experimental.pallas import tpu as pltpu

F32 = jnp.float32
BF16 = jnp.bfloat16

LANE = 128
SUBLANE = 8
MIB = 1024 * 1024

D_MODEL = 4096
DEPTH = 4
N_HEADS = 8
HEAD = 128
W_BR = 1024
CONV_W = 4
LRU_C = 8.0
N_BRANCH = 3
N_IN = 10 * W_BR + N_BRANCH * D_MODEL
N_EXPERTS = 8
LN_EPS = 1e-5
RMS_EPS = 1e-6
ALPHA = (2 * DEPTH) ** 0.25
ROPE_BASE = 10000.0
PAST_LEN = 16384

SEG_QA, SEG_FA, SEG_IA, SEG_GA, SEG_XB, SEG_GB, SEG_QC, SEG_KC, SEG_VC, SEG_GC = range(10)
GATE_COL0 = 10 * W_BR

GLA_CHUNK = 64
GLA_SUB = 16
NEG_BIG = -1e30


def _params(sem, vmem_mib):
    return pltpu.CompilerParams(dimension_semantics=sem, vmem_limit_bytes=vmem_mib * MIB)


def _dot(a, b):
    return jnp.dot(a, b, preferred_element_type=F32)


def _dot_nt(a, b):
    return lax.dot_general(a, b, (((1,), (1,)), ((), ())), preferred_element_type=F32)


def _dot_tn(a, b):
    return lax.dot_general(a, b, (((0,), (0,)), ((), ())), preferred_element_type=F32)


def _sigmoid(x):
    return 1.0 / (1.0 + jnp.exp(-x))


def _silu(x):
    return x * _sigmoid(x)


def _gelu_tanh(x):
    return 0.5 * x * (1.0 + jnp.tanh(0.7978845608028654 * (x + 0.044715 * (x * x * x))))


def _head_rms(o):
    return o * lax.rsqrt(jnp.mean(o * o, axis=-1, keepdims=True) + RMS_EPS)


def _lb_kernel(g_ref, o_ref):
    g = g_ref[...]
    rows = [g[j:j + 1, :] for j in range(DEPTH)]
    m = rows[0]
    for r in rows[1:]:
        m = jnp.maximum(m, r)
    e = [jnp.exp(r - m) for r in rows]
    tot = e[0]
    for x in e[1:]:
        tot = tot + x
    acc = jnp.zeros_like(m)
    for j in range(DEPTH):
        o_ref[j:j + 1, :] = acc
        acc = acc + e[j] / tot


def hgrn_lower_bounds(logits):
    return pl.pallas_call(_lb_kernel, out_shape=jax.ShapeDtypeStruct(logits.shape, F32), name="hgrn_lb")(logits)


def _mm_kernel(x_ref, w_ref, o_ref, wb_ref):
    @pl.when(pl.program_id(1) == 0)
    def _():
        wb_ref[...] = w_ref[...].astype(BF16)

    o_ref[...] = _dot(x_ref[...], wb_ref[...]).astype(o_ref.dtype)


def matmul(x, w_stack, layer, tm, tn, out_dtype=F32, name="matmul"):
    M, K = x.shape
    N = w_stack.shape[-1]
    assert M % tm == 0 and N % tn == 0
    return pl.pallas_call(
        _mm_kernel,
        grid=(N // tn, M // tm),
        in_specs=[pl.BlockSpec((tm, K), lambda j, i: (i, 0)),
                  pl.BlockSpec((None, K, tn), lambda j, i: (layer, 0, j))],
        out_specs=pl.BlockSpec((tm, tn), lambda j, i: (i, j)),
        out_shape=jax.ShapeDtypeStruct((M, N), out_dtype),
        scratch_shapes=[pltpu.VMEM((K, tn), BF16)],
        compiler_params=_params(("arbitrary", "arbitrary"), 56),
        name=name,
    )(x, w_stack)


def _swiglu_up_kernel(x_ref, w1_ref, w3_ref, o_ref, wb_ref, *, tn):
    @pl.when(pl.program_id(1) == 0)
    def _():
        wb_ref[:, :tn] = w1_ref[...].astype(BF16)
        wb_ref[:, tn:] = w3_ref[...].astype(BF16)

    ab = _dot(x_ref[...], wb_ref[...])
    o_ref[...] = (_silu(ab[:, :tn]) * ab[:, tn:]).astype(o_ref.dtype)


def swiglu_up(x, w1, w3, idx, tm, tn, name="swiglu_up"):
    M, K = x.shape
    FF = w1.shape[-1]
    nlead = len(idx)
    wspec = pl.BlockSpec((None,) * nlead + (K, tn), lambda j, i: idx + (0, j))
    return pl.pallas_call(
        functools.partial(_swiglu_up_kernel, tn=tn),
        grid=(pl.cdiv(FF, tn), M // tm),
        in_specs=[pl.BlockSpec((tm, K), lambda j, i: (i, 0)), wspec, wspec],
        out_specs=pl.BlockSpec((tm, tn), lambda j, i: (i, j)),
        out_shape=jax.ShapeDtypeStruct((M, FF), BF16),
        scratch_shapes=[pltpu.VMEM((K, 2 * tn), BF16)],
        compiler_params=_params(("arbitrary", "arbitrary"), 56),
        name=name,
    )(x, w1, w3)


def _down_kernel(h_ref, w_ref, o_ref, wb_ref):
    @pl.when(pl.program_id(1) == 0)
    def _():
        wb_ref[...] = w_ref[...].astype(BF16)

    o_ref[...] = _dot(h_ref[...], wb_ref[...])


def _down_acc_kernel(h_ref, w_ref, p_ref, o_ref, wb_ref):
    @pl.when(pl.program_id(1) == 0)
    def _():
        wb_ref[...] = w_ref[...].astype(BF16)

    o_ref[...] = p_ref[...] + _dot(h_ref[...], wb_ref[...])


K_HALVES = 2


def down_proj(h, w, layer, kb, tm, tn, prev=None, name="down"):
    M = h.shape[0]
    Kb = h.shape[1] // K_HALVES
    N = w.shape[-1]
    h_spec = pl.BlockSpec((tm, Kb), lambda j, i: (i, kb))
    w_spec = pl.BlockSpec((None, Kb, tn), lambda j, i: (layer, kb, j))
    o_spec = pl.BlockSpec((tm, tn), lambda j, i: (i, j))
    args, specs, alias = [h, w], [h_spec, w_spec], {}
    if prev is not None:
        body = _down_acc_kernel
        args.append(prev)
        specs.append(o_spec)
        alias = {2: 0}
    else:
        body = _down_kernel
    return pl.pallas_call(
        body,
        grid=(N // tn, M // tm),
        in_specs=specs,
        out_specs=o_spec,
        out_shape=jax.ShapeDtypeStruct((M, N), F32),
        scratch_shapes=[pltpu.VMEM((Kb, tn), BF16)],
        input_output_aliases=alias,
        compiler_params=_params(("arbitrary", "arbitrary"), 56),
        name=name,
    )(*args)


def _merge_kernel(ya_ref, yb_ref, yc_ref, w_ref, ga_ref, gb_ref, gc_ref, o_ref, wb_ref):
    @pl.when(pl.program_id(1) == 0)
    def _():
        wb_ref[...] = w_ref[...].astype(BF16)

    acc = _sigmoid(ga_ref[...]) * _dot(ya_ref[...], wb_ref[0])
    acc = acc + _sigmoid(gb_ref[...]) * _dot(yb_ref[...], wb_ref[1])
    acc = acc + _sigmoid(gc_ref[...]) * _dot(yc_ref[...], wb_ref[2])
    o_ref[...] = acc.astype(o_ref.dtype)


def branch_merge(ya, yb, yc, w_branch, layer, z, tm, tn):
    M = ya.shape[0]
    nt = D_MODEL // tn
    g0 = GATE_COL0 // tn
    y_spec = pl.BlockSpec((tm, W_BR), lambda j, i: (i, 0))

    def gate_spec(b):
        return pl.BlockSpec((tm, tn), lambda j, i: (i, g0 + b * nt + j))

    return pl.pallas_call(
        _merge_kernel,
        grid=(nt, M // tm),
        in_specs=[y_spec, y_spec, y_spec,
                  pl.BlockSpec((None, N_BRANCH, W_BR, tn), lambda j, i: (layer, 0, 0, j)),
                  gate_spec(0), gate_spec(1), gate_spec(2)],
        out_specs=pl.BlockSpec((tm, tn), lambda j, i: (i, j)),
        out_shape=jax.ShapeDtypeStruct((M, D_MODEL), BF16),
        scratch_shapes=[pltpu.VMEM((N_BRANCH, W_BR, tn), BF16)],
        compiler_params=_params(("arbitrary", "arbitrary"), 48),
        name="branch_merge",
    )(ya, yb, yc, w_branch, z, z, z)


def _ln_store(v, g_ref, b_ref, o_ref, ob_ref):
    mu = jnp.mean(v, axis=-1, keepdims=True)
    c = v - mu
    var = jnp.mean(c * c, axis=-1, keepdims=True)
    o = c * lax.rsqrt(var + LN_EPS) * g_ref[...] + b_ref[...]
    o_ref[...] = o
    ob_ref[...] = o.astype(BF16)


def _ln_kernel(x_ref, y_ref, g_ref, b_ref, o_ref, ob_ref):
    _ln_store(ALPHA * x_ref[...] + y_ref[...], g_ref, b_ref, o_ref, ob_ref)


def residual_ln(x, y, g, b, layer, tm):
    M = x.shape[0]
    row = pl.BlockSpec((tm, D_MODEL), lambda i: (i, 0))
    vec = pl.BlockSpec((None, 1, D_MODEL), lambda i: (layer, 0, 0))
    return pl.pallas_call(
        _ln_kernel,
        grid=(M // tm,),
        in_specs=[row, row, vec, vec],
        out_specs=[row, row],
        out_shape=[jax.ShapeDtypeStruct((M, D_MODEL), F32), jax.ShapeDtypeStruct((M, D_MODEL), BF16)],
        compiler_params=_params(("arbitrary",), 48),
        name="residual_ln",
    )(x, y, g.reshape(DEPTH, 1, D_MODEL), b.reshape(DEPTH, 1, D_MODEL))


def _cast_kernel(x_ref, o_ref):
    o_ref[...] = x_ref[...].astype(BF16)


def to_bf16(x, tm):
    M, N = x.shape
    spec = pl.BlockSpec((tm, N), lambda i: (i, 0))
    return pl.pallas_call(_cast_kernel, grid=(M // tm,), in_specs=[spec], out_specs=spec,
                          out_shape=jax.ShapeDtypeStruct((M, N), BF16),
                          compiler_params=_params(("arbitrary",), 32), name="to_bf16")(x)


TOP_K = 2
MOE_TM = 512


def _top1(cols):
    m = cols[0]
    for c in cols[1:]:
        m = jnp.maximum(m, c)
    idx = jnp.full(m.shape, N_EXPERTS, jnp.int32)
    for e in reversed(range(N_EXPERTS)):
        idx = jnp.where(cols[e] == m, e, idx)
    return m, idx


def _router_kernel(x_ref, rt_ref, idx_ref, w_ref):
    x = x_ref[...]
    cols = [jnp.sum(x * rt_ref[e:e + 1, :], axis=-1, keepdims=True) for e in range(N_EXPERTS)]
    m1, i1 = _top1(cols)
    m2, i2 = _top1([jnp.where(i1 == e, -jnp.inf, cols[e]) for e in range(N_EXPERTS)])
    e2 = jnp.exp(m2 - m1)
    den = 1.0 + e2
    first = lax.broadcasted_iota(jnp.int32, idx_ref.shape, 1) == 0
    idx_ref[...] = jnp.where(first, i1, i2)
    w_ref[...] = jnp.where(first, 1.0 / den, e2 / den)


def router_top2(x, router_t, idx, tm):
    M = x.shape[0]
    out = pl.BlockSpec((tm, TOP_K), lambda i: (i, 0))
    return pl.pallas_call(
        _router_kernel,
        grid=(M // tm,),
        in_specs=[pl.BlockSpec((tm, D_MODEL), lambda i: (i, 0)),
                  pl.BlockSpec((None, N_EXPERTS, D_MODEL), lambda i: (idx, 0, 0))],
        out_specs=[out, out],
        out_shape=[jax.ShapeDtypeStruct((M, TOP_K), jnp.int32), jax.ShapeDtypeStruct((M, TOP_K), F32)],
        compiler_params=_params(("arbitrary",), 48),
        name="router",
    )(x, router_t)


def moe_schedule(idx, wts):
    n_pairs = idx.size
    nt = n_pairs // MOE_TM + N_EXPERTS
    n_slots = nt * MOE_TM
    e_flat = idx.reshape(-1)
    onehot = (e_flat[:, None] == jnp.arange(N_EXPERTS, dtype=jnp.int32)[None, :]).astype(jnp.int32)
    csum = jnp.cumsum(onehot, axis=0)
    rank = jnp.sum((csum - onehot) * onehot, axis=1)
    padded = (csum[-1] + (MOE_TM - 1)) // MOE_TM * MOE_TM
    gend = jnp.cumsum(padded)
    slot = (gend - padded)[e_flat] + rank
    src = jnp.zeros((n_slots,), jnp.int32).at[slot].set(jnp.arange(n_pairs, dtype=jnp.int32) // TOP_K)
    wslot = jnp.zeros((n_slots,), F32).at[slot].set(wts.reshape(-1)).reshape(n_slots, 1)
    n_used = gend[-1] // MOE_TM
    tstart = jnp.arange(nt, dtype=jnp.int32) * MOE_TM
    tstart = jnp.minimum(tstart, (n_used - 1) * MOE_TM)
    tile_e = jnp.sum((tstart[:, None] >= gend[None, :]).astype(jnp.int32), axis=1)
    return src, wslot, slot.astype(jnp.int32), tile_e.astype(jnp.int32), n_used.reshape(1).astype(jnp.int32)


def _row_copy(src_hbm, row, dst, i, sem):
    return pltpu.make_async_copy(src_hbm.at[pl.ds(row, 1)], dst.at[pl.ds(i, 1)], sem)


def _gather_cast_kernel(src_ref, nu_ref, x_hbm, o_ref, buf, sem):
    t = pl.program_id(0)
    n_used = nu_ref[0]
    cur = t % 2

    def issue(tile, b):
        def body(i, c):
            _row_copy(x_hbm, src_ref[tile * MOE_TM + i], buf.at[b], i, sem.at[b]).start()
            return c

        lax.fori_loop(0, MOE_TM, body, 0)

    @pl.when(t == 0)
    def _():
        issue(0, 0)

    @pl.when(t + 1 < n_used)
    def _():
        issue(t + 1, 1 - cur)

    @pl.when(t < n_used)
    def _():
        def drain(i, c):
            _row_copy(x_hbm, 0, buf.at[cur], i, sem.at[cur]).wait()
            return c

        lax.fori_loop(0, MOE_TM, drain, 0)
        o_ref[...] = buf[cur].astype(BF16)


def _used_tile(t, nu_ref):
    return jnp.minimum(t, nu_ref[0] - 1)


def moe_gather(x, src, n_used):
    n_slots = src.shape[0]
    return pl.pallas_call(
        _gather_cast_kernel,
        grid_spec=pltpu.PrefetchScalarGridSpec(
            num_scalar_prefetch=2,
            grid=(n_slots // MOE_TM,),
            in_specs=[pl.BlockSpec(memory_space=pl.ANY)],
            out_specs=pl.BlockSpec((MOE_TM, D_MODEL), lambda t, src, nu: (_used_tile(t, nu), 0)),
            scratch_shapes=[pltpu.VMEM((2, MOE_TM, D_MODEL), F32), pltpu.SemaphoreType.DMA((2,))]),
        out_shape=jax.ShapeDtypeStruct((n_slots, D_MODEL), BF16),
        compiler_params=_params(("arbitrary",), 40),
        name="moe_gather",
    )(src, n_used, x)


def _new_expert(te_ref, t):
    return (t == 0) | (te_ref[t] != te_ref[jnp.maximum(t - 1, 0)])


def _moe_up_kernel(te_ref, nu_ref, x_ref, w1_ref, w3_ref, o_ref, wb_ref, *, tn):
    t = pl.program_id(1)

    @pl.when(_new_expert(te_ref, t))
    def _():
        wb_ref[:, :tn] = w1_ref[...].astype(BF16)
        wb_ref[:, tn:] = w3_ref[...].astype(BF16)

    @pl.when(t < nu_ref[0])
    def _():
        ab = _dot(x_ref[...], wb_ref[...])
        o_ref[...] = (_silu(ab[:, :tn]) * ab[:, tn:]).astype(o_ref.dtype)


def moe_up(xs, w1, w3, layer, tile_e, n_used, tn):
    n_slots, K = xs.shape
    FF = w1.shape[-1]
    wspec = pl.BlockSpec((None, None, K, tn), lambda j, t, te, nu: (layer, te[t], 0, j))
    return pl.pallas_call(
        functools.partial(_moe_up_kernel, tn=tn),
        grid_spec=pltpu.PrefetchScalarGridSpec(
            num_scalar_prefetch=2,
            grid=(pl.cdiv(FF, tn), n_slots // MOE_TM),
            in_specs=[pl.BlockSpec((MOE_TM, K), lambda j, t, te, nu: (_used_tile(t, nu), 0)), wspec, wspec],
            out_specs=pl.BlockSpec((MOE_TM, tn), lambda j, t, te, nu: (_used_tile(t, nu), j)),
            scratch_shapes=[pltpu.VMEM((K, 2 * tn), BF16)]),
        out_shape=jax.ShapeDtypeStruct((n_slots, FF), BF16),
        compiler_params=_params(("arbitrary", "arbitrary"), 56),
        name="moe_up",
    )(tile_e, n_used, xs, w1, w3)


def _moe_down_kernel(te_ref, nu_ref, h_ref, w_ref, g_ref, o_ref, wb_ref):
    t = pl.program_id(1)

    @pl.when(_new_expert(te_ref, t))
    def _():
        wb_ref[...] = w_ref[...].astype(BF16)

    @pl.when(t < nu_ref[0])
    def _():
        o_ref[...] = g_ref[...] * _dot(h_ref[...], wb_ref[...])


def moe_down(hs, w2, layer, wslot, tile_e, n_used, tn):
    n_slots, K = hs.shape
    N = w2.shape[-1]
    return pl.pallas_call(
        _moe_down_kernel,
        grid_spec=pltpu.PrefetchScalarGridSpec(
            num_scalar_prefetch=2,
            grid=(N // tn, n_slots // MOE_TM),
            in_specs=[pl.BlockSpec((MOE_TM, K), lambda j, t, te, nu: (_used_tile(t, nu), 0)),
                      pl.BlockSpec((None, None, K, tn), lambda j, t, te, nu: (layer, te[t], 0, j)),
                      pl.BlockSpec((MOE_TM, 1), lambda j, t, te, nu: (_used_tile(t, nu), 0))],
            out_specs=pl.BlockSpec((MOE_TM, tn), lambda j, t, te, nu: (_used_tile(t, nu), j)),
            scratch_shapes=[pltpu.VMEM((K, tn), BF16)]),
        out_shape=jax.ShapeDtypeStruct((n_slots, N), F32),
        compiler_params=_params(("arbitrary", "arbitrary"), 56),
        name="moe_down",
    )(tile_e, n_used, hs, w2, wslot)


def _ln_moe_kernel(slot_ref, x_ref, eo_hbm, g_ref, b_ref, o_ref, ob_ref, ya, yb, sem, *, tm):
    t = pl.program_id(0)
    cur = t % 2

    def issue(tile, b):
        def body(i, c):
            p = TOP_K * (tile * tm + i)
            _row_copy(eo_hbm, slot_ref[p], ya.at[b], i, sem.at[b]).start()
            _row_copy(eo_hbm, slot_ref[p + 1], yb.at[b], i, sem.at[b]).start()
            return c

        lax.fori_loop(0, tm, body, 0)

    @pl.when(t == 0)
    def _():
        issue(0, 0)

    @pl.when(t + 1 < pl.num_programs(0))
    def _():
        issue(t + 1, 1 - cur)

    def drain(i, c):
        _row_copy(eo_hbm, 0, ya.at[cur], i, sem.at[cur]).wait()
        _row_copy(eo_hbm, 0, yb.at[cur], i, sem.at[cur]).wait()
        return c

    lax.fori_loop(0, tm, drain, 0)
    _ln_store(ALPHA * x_ref[...] + (ya[cur] + yb[cur]), g_ref, b_ref, o_ref, ob_ref)


def residual_ln_moe(x, eo, slot, g, b, layer, tm):
    M = x.shape[0]
    row = pl.BlockSpec((tm, D_MODEL), lambda i, s: (i, 0))
    vec = pl.BlockSpec((None, 1, D_MODEL), lambda i, s: (layer, 0, 0))
    return pl.pallas_call(
        functools.partial(_ln_moe_kernel, tm=tm),
        grid_spec=pltpu.PrefetchScalarGridSpec(
            num_scalar_prefetch=1,
            grid=(M // tm,),
            in_specs=[row, pl.BlockSpec(memory_space=pl.ANY), vec, vec],
            out_specs=[row, row],
            scratch_shapes=[pltpu.VMEM((2, tm, D_MODEL), F32), pltpu.VMEM((2, tm, D_MODEL), F32),
                            pltpu.SemaphoreType.DMA((2,))]),
        out_shape=[jax.ShapeDtypeStruct((M, D_MODEL), F32), jax.ShapeDtypeStruct((M, D_MODEL), BF16)],
        compiler_params=_params(("arbitrary",), 48),
        name="residual_ln_moe",
    )(slot, x, eo, g.reshape(DEPTH, 1, D_MODEL), b.reshape(DEPTH, 1, D_MODEL))


def _hgrn_same_subblock(q, k, v, b):
    SB = GLA_SUB
    half = SB // 2
    rid = lax.broadcasted_iota(jnp.int32, (SB, HEAD), 0)
    rid_hi = lax.broadcasted_iota(jnp.int32, (half, HEAD), 0) + half
    outs = []
    for i in range(q.shape[0] // SB):
        lo = i * SB
        qi, bi, ki, vi = q[lo:lo + SB], b[lo:lo + SB], k[lo:lo + SB], v[lo:lo + SB]
        acc = jnp.zeros((SB, HEAD), F32)
        acc_hi = jnp.zeros((half, HEAD), F32)
        for s in range(half):
            d = jnp.where(rid >= s, bi - bi[s:s + 1], NEG_BIG)
            w = qi * jnp.exp2(d) * ki[s:s + 1]
            acc = acc + jnp.sum(w, axis=1, keepdims=True) * vi[s:s + 1]
        qh, bh = qi[half:], bi[half:]
        for s in range(half, SB):
            d = jnp.where(rid_hi >= s, bh - bi[s:s + 1], NEG_BIG)
            w = qh * jnp.exp2(d) * ki[s:s + 1]
            acc_hi = acc_hi + jnp.sum(w, axis=1, keepdims=True) * vi[s:s + 1]
        outs.append(acc[:half])
        outs.append(acc[half:] + acc_hi)
    return jnp.concatenate(outs, axis=0)


def _hgrn_prompt_kernel(q_ref, f_ref, i_ref, g_ref, lb_ref, y_ref, s_ref, st_ref, *, tb, hp):
    t = pl.program_id(2)

    @pl.when(t == 0)
    def _():
        st_ref[...] = jnp.zeros_like(st_ref)

    C, SB = GLA_CHUNK, GLA_SUB
    nc = tb // C
    lb = lb_ref[...]
    f = lb + (1.0 - lb) * _sigmoid(f_ref[...])
    r = lax.broadcasted_iota(jnp.int32, (tb, tb), 0)
    c = lax.broadcasted_iota(jnp.int32, (tb, tb), 1)
    tril = ((r >= c) & (r // C == c // C)).astype(F32)
    b_all = jnp.dot(tril, jnp.log2(f), preferred_element_type=F32, precision=lax.Precision.HIGHEST)
    k_all = 1.0 - f
    same_chunk = r // C == c // C
    rid = lax.broadcasted_iota(jnp.int32, (tb, HEAD), 0)
    sub = (rid % C) // SB
    chunk = rid // C

    def row_bcast(x, row_of_chunk):
        return jnp.concatenate([jnp.broadcast_to(x[ci * C + row_of_chunk:ci * C + row_of_chunk + 1], (C, HEAD))
                                for ci in range(nc)], axis=0)

    for h in range(hp):
        cols = slice(h * HEAD, (h + 1) * HEAD)
        q, v = q_ref[:, cols], i_ref[:, cols]
        k, b = k_all[:, cols], b_all[:, cols]
        vb = v.astype(BF16)
        o = _hgrn_same_subblock(q, k, v, b)

        betas = [row_bcast(b, i * SB - 1) for i in range(1, C // SB)]
        beta_row = betas[-1]
        for i in range(C // SB - 2, 0, -1):
            beta_row = jnp.where(sub == i, betas[i - 1], beta_row)
        qd = q * jnp.exp2(jnp.where(sub > 0, b - beta_row, NEG_BIG))
        q_parts = [jnp.where(sub == i, qd, 0.0) for i in range(1, C // SB)]
        k_parts = [k * jnp.exp2(jnp.where(sub < i, betas[i - 1] - b, NEG_BIG)) for i in range(1, C // SB)]
        scores = _dot_nt(jnp.concatenate(q_parts, axis=1).astype(BF16), jnp.concatenate(k_parts, axis=1).astype(BF16))
        o = o + _dot(jnp.where(same_chunk, scores, 0.0).astype(BF16), vb)

        b_last = row_bcast(b, C - 1)
        kt = (k * jnp.exp2(b_last - b)).astype(BF16)
        zero = jnp.zeros_like(kt)
        upd = _dot_tn(vb, jnp.concatenate([jnp.where(chunk == ci, kt, zero) for ci in range(nc)], axis=1))
        st = st_ref[h]
        starts = []
        for ci in range(nc):
            starts.append(st.astype(BF16))
            st = jnp.exp2(b[ci * C + C - 1:ci * C + C]) * st + upd[:, ci * HEAD:(ci + 1) * HEAD]
        st_ref[h] = st
        inter = _dot_nt((q * jnp.exp2(b)).astype(BF16), jnp.concatenate(starts, axis=0))
        o = o + jnp.concatenate([inter[ci * C:(ci + 1) * C, ci * HEAD:(ci + 1) * HEAD] for ci in range(nc)], axis=0)
        y_ref[:, cols] = (_head_rms(o) * _silu(g_ref[:, cols])).astype(BF16)

    @pl.when(t == pl.num_programs(2) - 1)
    def _():
        for h in range(hp):
            s_ref[h] = st_ref[h].T


def hgrn_prompt(z, lb, nb, T, tb, hp, m_total):
    nt = T // tb
    ng = N_HEADS // hp

    def seg(s):
        return pl.BlockSpec((tb, hp * HEAD), lambda b, h, t: (b * nt + t, s * ng + h))

    return pl.pallas_call(
        functools.partial(_hgrn_prompt_kernel, tb=tb, hp=hp),
        grid=(nb, ng, nt),
        in_specs=[seg(SEG_QA), seg(SEG_FA), seg(SEG_IA), seg(SEG_GA),
                  pl.BlockSpec((1, hp * HEAD), lambda b, h, t: (0, h))],
        out_specs=[pl.BlockSpec((tb, hp * HEAD), lambda b, h, t: (b * nt + t, h)),
                   pl.BlockSpec((None, hp, HEAD, HEAD), lambda b, h, t: (b, h, 0, 0))],
        out_shape=[jax.ShapeDtypeStruct((m_total, W_BR), BF16),
                   jax.ShapeDtypeStruct((nb, N_HEADS, HEAD, HEAD), F32)],
        scratch_shapes=[pltpu.VMEM((hp, HEAD, HEAD), F32)],
        compiler_params=_params(("arbitrary", "arbitrary", "arbitrary"), 32),
        name="hgrn_prompt",
    )(z, z, z, z, lb)


def _rotate(x, cosf, sinf):
    return x * cosf + pltpu.roll(x, HEAD // 2, 1) * sinf


def _ret_prompt_kernel(q_ref, k_ref, v_ref, g_ref, cos_ref, sin_ref, lg_ref, y_ref, s_ref, st_ref, *, tb, hp):
    t = pl.program_id(2)

    @pl.when(t == 0)
    def _():
        st_ref[...] = jnp.zeros_like(st_ref)

    cosf, sinf = cos_ref[...], sin_ref[...]
    tpos = lax.broadcasted_iota(jnp.int32, (tb, HEAD), 0).astype(F32)
    r = lax.broadcasted_iota(jnp.int32, (tb, tb), 0)
    c = lax.broadcasted_iota(jnp.int32, (tb, tb), 1)
    causal = r >= c
    lag = (r - c).astype(F32)
    for h in range(hp):
        cols = slice(h * HEAD, (h + 1) * HEAD)
        lg_w = lg_ref[h]
        lg = lg_w[:, :HEAD]
        qr = _rotate(q_ref[:, cols], cosf, sinf)
        kr = _rotate(k_ref[:, cols], cosf, sinf) * (HEAD ** -0.5)
        vb = v_ref[:, cols].astype(BF16)
        dmat = jnp.where(causal, jnp.exp(lag * lg_w), 0.0)
        st = st_ref[h]
        scores = _dot_nt(qr.astype(BF16), kr.astype(BF16)) * dmat
        o = _dot(scores.astype(BF16), vb)
        o = o + _dot_nt((qr * jnp.exp((tpos + 1.0) * lg)).astype(BF16), st.astype(BF16))
        kt = (kr * jnp.exp((tb - 1.0 - tpos) * lg)).astype(BF16)
        st_ref[h] = jnp.exp(float(tb) * lg) * st + _dot_tn(vb, kt)
        y_ref[:, cols] = (_head_rms(o) * _silu(g_ref[:, cols])).astype(BF16)

    @pl.when(t == pl.num_programs(2) - 1)
    def _():
        for h in range(hp):
            s_ref[h] = st_ref[h].T


def _rope_tables(pos):
    half = HEAD // 2
    inv_freq = 1.0 / (ROPE_BASE ** jnp.linspace(0.0, 1.0, half, dtype=F32))
    ang = pos.astype(F32)[:, None] * inv_freq[None, :]
    cos, sin = jnp.cos(ang), jnp.sin(ang)
    return jnp.concatenate([cos, cos], axis=-1), jnp.concatenate([-sin, sin], axis=-1)


def _log_gamma():
    return jnp.log1p(-jnp.power(2.0, -5.0 - jnp.arange(N_HEADS, dtype=F32)))


def ret_prompt(z, nb, T, tb, hp, m_total):
    nt = T // tb
    ng = N_HEADS // hp
    cosf, sinf = _rope_tables(jnp.arange(T))
    lg = jnp.broadcast_to(_log_gamma()[:, None, None], (N_HEADS, 1, tb))

    def seg(s):
        return pl.BlockSpec((tb, hp * HEAD), lambda b, h, t: (b * nt + t, s * ng + h))

    tab = pl.BlockSpec((tb, HEAD), lambda b, h, t: (t, 0))
    return pl.pallas_call(
        functools.partial(_ret_prompt_kernel, tb=tb, hp=hp),
        grid=(nb, ng, nt),
        in_specs=[seg(SEG_QC), seg(SEG_KC), seg(SEG_VC), seg(SEG_GC), tab, tab,
                  pl.BlockSpec((hp, 1, tb), lambda b, h, t: (h, 0, 0))],
        out_specs=[pl.BlockSpec((tb, hp * HEAD), lambda b, h, t: (b * nt + t, h)),
                   pl.BlockSpec((None, hp, HEAD, HEAD), lambda b, h, t: (b, h, 0, 0))],
        out_shape=[jax.ShapeDtypeStruct((m_total, W_BR), BF16),
                   jax.ShapeDtypeStruct((nb, N_HEADS, HEAD, HEAD), F32)],
        scratch_shapes=[pltpu.VMEM((hp, HEAD, HEAD), F32)],
        compiler_params=_params(("arbitrary", "arbitrary", "arbitrary"), 32),
        name="ret_prompt",
    )(z, z, z, z, cosf, sinf, lg)


def _lru_gates(xc, wa_ref, ba_ref, wx_ref, bx_ref, lam_ref):
    xcb = xc.astype(BF16)
    ra, ia = [], []
    for n in range(W_BR // HEAD):
        xn = xcb[:, n * HEAD:(n + 1) * HEAD]
        ra.append(_dot(xn, wa_ref[n].astype(BF16)))
        ia.append(_dot(xn, wx_ref[n].astype(BF16)))
    r = _sigmoid(jnp.concatenate(ra, axis=1) + ba_ref[...])
    i = _sigmoid(jnp.concatenate(ia, axis=1) + bx_ref[...])
    nlam = -lam_ref[...]
    softplus = jnp.maximum(nlam, 0.0) + jnp.log1p(jnp.exp(-jnp.abs(nlam)))
    log_a = -LRU_C * r * softplus
    a = jnp.exp(log_a)
    th = jnp.tanh(log_a)
    one_m_a2 = -2.0 * th / (1.0 - th)
    root = jnp.where(one_m_a2 > 0.0, one_m_a2 * lax.rsqrt(one_m_a2), 0.0)
    u = root * (i * xc)
    return a, u


def _lru_prompt_kernel(x_ref, g_ref, cw_ref, cb_ref, wa_ref, ba_ref, wx_ref, bx_ref, lam_ref,
                       y_ref, h_ref, conv_ref, prev_ref, hc_ref, *, tb):
    t = pl.program_id(1)

    @pl.when(t == 0)
    def _():
        prev_ref[...] = jnp.zeros_like(prev_ref)
        hc_ref[...] = jnp.zeros_like(hc_ref)

    x = x_ref[...]
    prev = prev_ref[...]
    rid = lax.broadcasted_iota(jnp.int32, (tb, W_BR), 0)
    rid8 = lax.broadcasted_iota(jnp.int32, (SUBLANE, W_BR), 0)
    cw = cw_ref[...]
    xc = cb_ref[...] + cw[CONV_W - 1:CONV_W] * x
    for j in range(1, CONV_W):
        xr = pltpu.roll(x, j, 0)
        head = jnp.where(rid8 < j, pltpu.roll(prev, j, 0), xr[:SUBLANE])
        xc = xc + cw[CONV_W - 1 - j:CONV_W - j] * jnp.concatenate([head, xr[SUBLANE:]], axis=0)
    a, u = _lru_gates(xc, wa_ref, ba_ref, wx_ref, bx_ref, lam_ref)
    row_in_group = rid % SUBLANE
    d = 1
    while d < SUBLANE:
        keep = row_in_group >= d
        a_s = jnp.where(keep, pltpu.roll(a, d, 0), 1.0)
        u_s = jnp.where(keep, pltpu.roll(u, d, 0), 0.0)
        u = a * u_s + u
        a = a * a_s
        d *= 2
    carry = hc_ref[0:1, :]
    groups = []
    for g in range(tb // SUBLANE):
        rows = slice(g * SUBLANE, (g + 1) * SUBLANE)
        hg = a[rows] * carry + u[rows]
        groups.append(hg)
        carry = hg[SUBLANE - 1:SUBLANE]
    h = jnp.concatenate(groups, axis=0)
    y_ref[...] = (h * _gelu_tanh(g_ref[...])).astype(BF16)
    hc_ref[0:1, :] = h[tb - 1:tb]
    prev_ref[...] = x[tb - SUBLANE:tb]

    @pl.when(t == pl.num_programs(1) - 1)
    def _():
        h_ref[...] = h[tb - 1:tb]
        conv_ref[...] = x[tb - (CONV_W - 1):tb]


def _lru_weight_specs(layer):
    def vec():
        return pl.BlockSpec((None, 1, W_BR), lambda *_: (layer, 0, 0))

    def blk():
        return pl.BlockSpec((None, W_BR // HEAD, HEAD, HEAD), lambda *_: (layer, 0, 0, 0))

    return [pl.BlockSpec((None, CONV_W, W_BR), lambda *_: (layer, 0, 0)), vec(), blk(), vec(), blk(), vec(), vec()]


def _lru_weight_args(p):
    v = lambda a: a.reshape(DEPTH, 1, W_BR)
    return [p["lru_conv_w"], v(p["lru_conv_b"]), p["lru_wa"], v(p["lru_ba"]), p["lru_wx"], v(p["lru_bx"]),
            v(p["lru_lambda"])]


def lru_prompt(z, p, layer, nb, T, tb, m_total):
    nt = T // tb
    outs = pl.pallas_call(
        functools.partial(_lru_prompt_kernel, tb=tb),
        grid=(nb, nt),
        in_specs=[pl.BlockSpec((tb, W_BR), lambda b, t: (b * nt + t, SEG_XB)),
                  pl.BlockSpec((tb, W_BR), lambda b, t: (b * nt + t, SEG_GB))] + _lru_weight_specs(layer),
        out_specs=[pl.BlockSpec((tb, W_BR), lambda b, t: (b * nt + t, 0)),
                   pl.BlockSpec((None, 1, W_BR), lambda b, t: (b, 0, 0)),
                   pl.BlockSpec((None, CONV_W - 1, W_BR), lambda b, t: (b, 0, 0))],
        out_shape=[jax.ShapeDtypeStruct((m_total, W_BR), BF16),
                   jax.ShapeDtypeStruct((nb, 1, W_BR), F32),
                   jax.ShapeDtypeStruct((nb, CONV_W - 1, W_BR), F32)],
        scratch_shapes=[pltpu.VMEM((SUBLANE, W_BR), F32), pltpu.VMEM((SUBLANE, W_BR), F32)],
        compiler_params=_params(("arbitrary", "arbitrary"), 48),
        name="lru_prompt",
    )(z, z, *_lru_weight_args(p))
    return outs[0], outs[1].reshape(nb, W_BR), outs[2]


def _lru_sample_kernel(x_ref, g_ref, buf_ref, h0_ref, cw_ref, cb_ref, wa_ref, ba_ref, wx_ref, bx_ref, lam_ref,
                       y_in_ref, y_ref, h_ref, conv_ref):
    del y_in_ref
    x = x_ref[...]
    cw = cw_ref[...]
    xc = cb_ref[...] + cw[CONV_W - 1:CONV_W] * x
    for j in range(CONV_W - 1):
        xc = xc + cw[j:j + 1] * buf_ref[j]
    a, u = _lru_gates(xc, wa_ref, ba_ref, wx_ref, bx_ref, lam_ref)
    h = a * h0_ref[...] + u
    y_ref[...] = (h * _gelu_tanh(g_ref[...])).astype(BF16)
    h_ref[...] = h
    for j in range(CONV_W - 2):
        conv_ref[j] = buf_ref[j + 1]
    conv_ref[CONV_W - 2] = x


def lru_sample(z, p, layer, row_blk, ns, buf_t, h0, y):
    full3 = pl.BlockSpec((CONV_W - 1, ns, W_BR), lambda i: (0, 0, 0))
    full2 = pl.BlockSpec((ns, W_BR), lambda i: (0, 0))
    n_in = 4 + len(_lru_weight_specs(layer))
    return pl.pallas_call(
        _lru_sample_kernel,
        grid=(1,),
        in_specs=[pl.BlockSpec((ns, W_BR), lambda i: (row_blk, SEG_XB)),
                  pl.BlockSpec((ns, W_BR), lambda i: (row_blk, SEG_GB)), full3, full2] + _lru_weight_specs(layer)
                 + [pl.BlockSpec(memory_space=pl.ANY)],
        out_specs=[pl.BlockSpec((ns, W_BR), lambda i: (row_blk, 0)), full2, full3],
        out_shape=[jax.ShapeDtypeStruct(y.shape, BF16), jax.ShapeDtypeStruct((ns, W_BR), F32),
                   jax.ShapeDtypeStruct((CONV_W - 1, ns, W_BR), F32)],
        input_output_aliases={n_in: 0},
        compiler_params=_params(("arbitrary",), 32),
        name="lru_sample",
    )(z, z, buf_t, h0, *_lru_weight_args(p), y)


GRP = SUBLANE


def _channel_major(zs, seg):
    ns = zs.shape[0]
    x = zs[:, seg * W_BR:(seg + 1) * W_BR].reshape(ns // GRP, GRP, N_HEADS, HEAD)
    return x.transpose(0, 2, 3, 1)


def _hgrn_sample_kernel(f_ref, lb_ref, q_ref, v_ref, g_ref, s_ref, *aliased_and_outs):
    y_ref, so_ref, o_scr = aliased_and_outs[-3:]
    for h in range(N_HEADS):
        cols = slice(h * HEAD, (h + 1) * HEAD)
        lb = lb_ref[h]
        f = lb + (1.0 - lb) * _sigmoid(f_ref[h])
        qb = q_ref[:, cols].astype(BF16)
        for j in range(GRP):
            vrow = v_ref[j:j + 1, cols]
            sn = f[:, j:j + 1] * (s_ref[j, h] - vrow) + vrow
            so_ref[j, h] = sn
            o_scr[j:j + 1, cols] = _dot(qb, sn.astype(BF16))[j:j + 1]
    for h in range(N_HEADS):
        cols = slice(h * HEAD, (h + 1) * HEAD)
        y_ref[:, cols] = (_head_rms(o_scr[:, cols]) * _silu(g_ref[:, cols])).astype(BF16)


def _sample_state_call(body, name, lead_args, lead_specs, z, segs, states, layer, row0, y, stacked):
    ns = states.shape[1]
    rb = row0 // GRP
    st = pl.BlockSpec((None, GRP, N_HEADS, HEAD, HEAD), lambda g: (layer, g, 0, 0, 0))
    anyspec = pl.BlockSpec(memory_space=pl.ANY)
    in_specs = lead_specs + [pl.BlockSpec((GRP, W_BR), lambda g, s=s: (rb + g, s)) for s in segs] + [st, anyspec]
    args = lead_args + [z] * len(segs) + [states, y]
    alias = {len(args) - 1: 0}
    if stacked is not None:
        in_specs.append(anyspec)
        args.append(stacked)
        alias[len(args) - 1] = 1
    return pl.pallas_call(
        body,
        grid=(ns // GRP,),
        in_specs=in_specs,
        out_specs=[pl.BlockSpec((GRP, W_BR), lambda g: (rb + g, 0)), st],
        out_shape=[jax.ShapeDtypeStruct(y.shape, BF16), jax.ShapeDtypeStruct(states.shape, F32)],
        scratch_shapes=[pltpu.VMEM((GRP, W_BR), F32)],
        input_output_aliases=alias,
        compiler_params=_params(("arbitrary",), 48),
        name=name,
    )(*args)


def hgrn_sample(z, zs, lb, states, layer, row0, y, stacked):
    cm = pl.BlockSpec((None, N_HEADS, HEAD, GRP), lambda g: (g, 0, 0, 0))
    lead_specs = [cm, pl.BlockSpec((N_HEADS, HEAD, 1), lambda g: (0, 0, 0))]
    lead_args = [_channel_major(zs, SEG_FA), lb.reshape(N_HEADS, HEAD, 1)]
    return _sample_state_call(_hgrn_sample_kernel, "hgrn_sample", lead_args, lead_specs, z,
                              (SEG_QA, SEG_IA, SEG_GA), states, layer, row0, y, stacked)


def _ret_sample_kernel(k_ref, cos_ref, sin_ref, cosf_ref, sinf_ref, gam_ref, q_ref, v_ref, g_ref, s_ref,
                       *aliased_and_outs):
    y_ref, so_ref, o_scr = aliased_and_outs[-3:]
    half = HEAD // 2
    cosc, sinc = cos_ref[...], sin_ref[...]
    cosf, sinf = cosf_ref[...], sinf_ref[...]

    def rot(x):
        x1, x2 = x[:half], x[half:]
        return jnp.concatenate([x1 * cosc - x2 * sinc, x1 * sinc + x2 * cosc], axis=0)

    for h in range(N_HEADS):
        cols = slice(h * HEAD, (h + 1) * HEAD)
        k = rot(k_ref[h]) * (HEAD ** -0.5)
        qb = _rotate(q_ref[:, cols], cosf, sinf).astype(BF16)
        gam = gam_ref[h]
        for j in range(GRP):
            sn = gam * s_ref[j, h] + k[:, j:j + 1] * v_ref[j:j + 1, cols]
            so_ref[j, h] = sn
            o_scr[j:j + 1, cols] = _dot(qb, sn.astype(BF16))[j:j + 1]
    for h in range(N_HEADS):
        cols = slice(h * HEAD, (h + 1) * HEAD)
        y_ref[:, cols] = (_head_rms(o_scr[:, cols]) * _silu(g_ref[:, cols])).astype(BF16)


def ret_sample(z, zs, states, layer, row0, y, stacked):
    half = HEAD // 2
    cosf, sinf = _rope_tables(jnp.full((1,), PAST_LEN))
    cosc = cosf[0, :half].reshape(half, 1)
    sinc = sinf[0, half:].reshape(half, 1)
    gam = jnp.broadcast_to(jnp.exp(_log_gamma())[:, None, None], (N_HEADS, 1, HEAD))
    cm = pl.BlockSpec((None, N_HEADS, HEAD, GRP), lambda g: (g, 0, 0, 0))
    col = pl.BlockSpec((half, 1), lambda g: (0, 0))
    row = pl.BlockSpec((1, HEAD), lambda g: (0, 0))
    lead_specs = [cm, col, col, row, row, pl.BlockSpec((N_HEADS, 1, HEAD), lambda g: (0, 0, 0))]
    lead_args = [_channel_major(zs, SEG_KC), cosc, sinc, cosf, sinf, gam]
    return _sample_state_call(_ret_sample_kernel, "ret_sample", lead_args, lead_specs, z,
                              (SEG_QC, SEG_VC, SEG_GC), states, layer, row0, y, stacked)


TM = 1040
TM_DOWN = 520
TM_ROW = 208
TB_SEQ = 256
HEADS_PER_STEP = 8


def kernel(x_prompt, x_sample, state_hgrn, state_lru_h, state_lru_conv, state_ret, w_in, hgrn_lb_logits, lru_conv_w, lru_conv_b, lru_wa, lru_ba, lru_wx, lru_bx, lru_lambda, w_branch, w_out, ln1_g, ln1_b, ln2_g, ln2_b, ffn_w1, ffn_w3, ffn_w2, router, moe_w1, moe_w3, moe_w2):
    nb, T, _ = x_prompt.shape
    ns = x_sample.shape[0]
    mp = nb * T
    lru_p = dict(lru_conv_w=lru_conv_w, lru_conv_b=lru_conv_b, lru_wa=lru_wa, lru_ba=lru_ba,
                 lru_wx=lru_wx, lru_bx=lru_bx, lru_lambda=lru_lambda)

    lb_all = hgrn_lower_bounds(hgrn_lb_logits)
    x = jnp.concatenate([x_prompt.reshape(mp, D_MODEL), x_sample.reshape(ns, D_MODEL)], axis=0)
    xb = to_bf16(x, TM_ROW)
    router_t = router.transpose(0, 2, 1)

    m_total = mp + ns
    new_p, new_s = [], []
    hgrn_s = ret_s = None
    for l in range(DEPTH):
        z = matmul(xb, w_in, l, TM, 512, name="in_proj")
        zs = z[mp:]
        lb = lb_all[l:l + 1]

        ya, hgrn_p = hgrn_prompt(z, lb, nb, T, TB_SEQ, HEADS_PER_STEP, m_total)
        yb, lru_h_p, conv_p = lru_prompt(z, lru_p, l, nb, T, TB_SEQ, m_total)
        yc, ret_p = ret_prompt(z, nb, T, TB_SEQ, HEADS_PER_STEP, m_total)

        ya, hgrn_s = hgrn_sample(z, zs, lb, state_hgrn, l, mp, ya, hgrn_s)
        yb, lru_h_s, conv_s = lru_sample(z, lru_p, l, mp // ns, ns, state_lru_conv[l].transpose(1, 0, 2),
                                         state_lru_h[l], yb)
        yc, ret_s = ret_sample(z, zs, state_ret, l, mp, yc, ret_s)
        new_p.append((hgrn_p, lru_h_p, conv_p, ret_p))
        new_s.append((lru_h_s, conv_s.transpose(1, 0, 2)))

        merged = branch_merge(ya, yb, yc, w_branch, l, z, TM, 512)
        mix = matmul(merged, w_out, l, TM, 512, name="out_proj")
        x, xb = residual_ln(x, mix, ln1_g, ln1_b, l, TM_ROW)

        if l % 2 == 0:
            h = swiglu_up(xb, ffn_w1, ffn_w3, (l // 2,), TM, 256, name="ffn_up")
            y = down_proj(h, ffn_w2, l // 2, 0, TM_DOWN, 512, name="ffn_down")
            y = down_proj(h, ffn_w2, l // 2, 1, TM_DOWN, 512, prev=y, name="ffn_down")
            x, xb = residual_ln(x, y, ln2_g, ln2_b, l, TM_ROW)
        else:
            eidx, ew = router_top2(x, router_t, l // 2, TM_DOWN)
            src, wslot, slot, tile_e, n_used = moe_schedule(eidx, ew)
            xs = moe_gather(x, src, n_used)
            hs = moe_up(xs, moe_w1, moe_w3, l // 2, tile_e, n_used, 512)
            eo = moe_down(hs, moe_w2, l // 2, wslot, tile_e, n_used, 512)
            x, xb = residual_ln_moe(x, eo, slot, ln2_g, ln2_b, l, TM_ROW)

    def stack(states, j):
        return jnp.stack([s[j] for s in states], axis=0)

    return (x[:mp].reshape(nb, T, D_MODEL), x[mp:].reshape(ns, 1, D_MODEL),
            stack(new_p, 0), stack(new_p, 1), stack(new_p, 2), stack(new_p, 3),
            hgrn_s, stack(new_s, 0), stack(new_s, 1), ret_s)
```

```python
import functools

import jax
import jax.numpy as jnp
from jax import lax
from jax.experimental import pallas as pl
from jax.experimental.pallas import tpu as pltpu

F32 = jnp.float32
BF16 = jnp.bfloat16

LANE = 128
SUBLANE = 8
MIB = 1024 * 1024

D_MODEL = 4096
DEPTH = 4
N_HEADS = 8
HEAD = 128
W_BR = 1024
CONV_W = 4
LRU_C = 8.0
N_BRANCH = 3
N_IN = 10 * W_BR + N_BRANCH * D_MODEL
N_EXPERTS = 8
LN_EPS = 1e-5
RMS_EPS = 1e-6
ALPHA = (2 * DEPTH) ** 0.25
ROPE_BASE = 10000.0
PAST_LEN = 16384

SEG_QA, SEG_FA, SEG_IA, SEG_GA, SEG_XB, SEG_GB, SEG_QC, SEG_KC, SEG_VC, SEG_GC = range(10)
GATE_COL0 = 10 * W_BR

GLA_CHUNK = 64
GLA_SUB = 16
NEG_BIG = -1e30


def _params(sem, vmem_mib):
    return pltpu.CompilerParams(dimension_semantics=sem, vmem_limit_bytes=vmem_mib * MIB)


def _dot(a, b):
    return jnp.dot(a, b, preferred_element_type=F32)


def _dot_nt(a, b):
    return lax.dot_general(a, b, (((1,), (1,)), ((), ())), preferred_element_type=F32)


def _dot_tn(a, b):
    return lax.dot_general(a, b, (((0,), (0,)), ((), ())), preferred_element_type=F32)


def _sigmoid(x):
    return 1.0 / (1.0 + jnp.exp(-x))


def _silu(x):
    return x * _sigmoid(x)


def _gelu_tanh(x):
    return 0.5 * x * (1.0 + jnp.tanh(0.7978845608028654 * (x + 0.044715 * (x * x * x))))


def _head_rms(o):
    return o * lax.rsqrt(jnp.mean(o * o, axis=-1, keepdims=True) + RMS_EPS)


def _lb_kernel(g_ref, o_ref):
    g = g_ref[...]
    rows = [g[j:j + 1, :] for j in range(DEPTH)]
    m = rows[0]
    for r in rows[1:]:
        m = jnp.maximum(m, r)
    e = [jnp.exp(r - m) for r in rows]
    tot = e[0]
    for x in e[1:]:
        tot = tot + x
    acc = jnp.zeros_like(m)
    for j in range(DEPTH):
        o_ref[j:j + 1, :] = acc
        acc = acc + e[j] / tot


def hgrn_lower_bounds(logits):
    return pl.pallas_call(_lb_kernel, out_shape=jax.ShapeDtypeStruct(logits.shape, F32), name="hgrn_lb")(logits)


def _mm_staged_kernel(x_ref, w_hbm, o_ref, stage_ref, wb_ref, sem, *, layer, tn):
    j, i = pl.program_id(0), pl.program_id(1)

    def tile_copy(jt):
        return pltpu.make_async_copy(w_hbm.at[layer, :, pl.ds(jt * tn, tn)], stage_ref, sem)

    @pl.when((j == 0) & (i == 0))
    def _():
        tile_copy(0).start()

    @pl.when(i == 0)
    def _():
        tile_copy(j).wait()
        wb_ref[...] = stage_ref[...].astype(BF16)

    @pl.when((i == 1) & (j + 1 < pl.num_programs(0)))
    def _():
        tile_copy(j + 1).start()

    o_ref[...] = _dot(x_ref[...], wb_ref[...]).astype(o_ref.dtype)


def matmul(x, w_stack, layer, tm, tn, out_dtype=F32, name="matmul"):
    M, K = x.shape
    N = w_stack.shape[-1]
    assert M % tm == 0 and N % tn == 0 and M // tm >= 2
    return pl.pallas_call(
        functools.partial(_mm_staged_kernel, layer=layer, tn=tn),
        grid=(N // tn, M // tm),
        in_specs=[pl.BlockSpec((tm, K), lambda j, i: (i, 0)),
                  pl.BlockSpec(memory_space=pl.ANY)],
        out_specs=pl.BlockSpec((tm, tn), lambda j, i: (i, j)),
        out_shape=jax.ShapeDtypeStruct((M, N), out_dtype),
        scratch_shapes=[pltpu.VMEM((K, tn), F32), pltpu.VMEM((K, tn), BF16), pltpu.SemaphoreType.DMA(())],
        compiler_params=_params(("arbitrary", "arbitrary"), 58),
        name=name,
    )(x, w_stack)


def _swiglu_up_kernel(x_ref, w1_ref, w3_ref, o_ref, wb_ref, *, tn):
    @pl.when(pl.program_id(1) == 0)
    def _():
        wb_ref[:, :tn] = w1_ref[...].astype(BF16)
        wb_ref[:, tn:] = w3_ref[...].astype(BF16)

    ab = _dot(x_ref[...], wb_ref[...])
    o_ref[...] = (_silu(ab[:, :tn]) * ab[:, tn:]).astype(o_ref.dtype)


def swiglu_up(x, w1, w3, idx, tm, tn, name="swiglu_up"):
    M, K = x.shape
    FF = w1.shape[-1]
    nlead = len(idx)
    wspec = pl.BlockSpec((None,) * nlead + (K, tn), lambda j, i: idx + (0, j))
    return pl.pallas_call(
        functools.partial(_swiglu_up_kernel, tn=tn),
        grid=(pl.cdiv(FF, tn), M // tm),
        in_specs=[pl.BlockSpec((tm, K), lambda j, i: (i, 0)), wspec, wspec],
        out_specs=pl.BlockSpec((tm, tn), lambda j, i: (i, j)),
        out_shape=jax.ShapeDtypeStruct((M, FF), BF16),
        scratch_shapes=[pltpu.VMEM((K, 2 * tn), BF16)],
        compiler_params=_params(("arbitrary", "arbitrary"), 56),
        name=name,
    )(x, w1, w3)


def _down_kernel(h_ref, w_ref, o_ref, wb_ref):
    @pl.when(pl.program_id(1) == 0)
    def _():
        wb_ref[...] = w_ref[...].astype(BF16)

    o_ref[...] = _dot(h_ref[...], wb_ref[...])


def _down_acc_kernel(h_ref, w_ref, p_ref, o_ref, wb_ref):
    @pl.when(pl.program_id(1) == 0)
    def _():
        wb_ref[...] = w_ref[...].astype(BF16)

    o_ref[...] = p_ref[...] + _dot(h_ref[...], wb_ref[...])


K_HALVES = 2


def down_proj(h, w, layer, kb, tm, tn, prev=None, name="down"):
    M = h.shape[0]
    Kb = h.shape[1] // K_HALVES
    N = w.shape[-1]
    h_spec = pl.BlockSpec((tm, Kb), lambda j, i: (i, kb))
    w_spec = pl.BlockSpec((None, Kb, tn), lambda j, i: (layer, kb, j))
    o_spec = pl.BlockSpec((tm, tn), lambda j, i: (i, j))
    args, specs, alias = [h, w], [h_spec, w_spec], {}
    if prev is not None:
        body = _down_acc_kernel
        args.append(prev)
        specs.append(o_spec)
        alias = {2: 0}
    else:
        body = _down_kernel
    return pl.pallas_call(
        body,
        grid=(N // tn, M // tm),
        in_specs=specs,
        out_specs=o_spec,
        out_shape=jax.ShapeDtypeStruct((M, N), F32),
        scratch_shapes=[pltpu.VMEM((Kb, tn), BF16)],
        input_output_aliases=alias,
        compiler_params=_params(("arbitrary", "arbitrary"), 56),
        name=name,
    )(*args)


def _merge_kernel(ya_ref, yb_ref, yc_ref, w_ref, ga_ref, gb_ref, gc_ref, o_ref, wb_ref):
    @pl.when(pl.program_id(1) == 0)
    def _():
        wb_ref[...] = w_ref[...].astype(BF16)

    acc = _sigmoid(ga_ref[...]) * _dot(ya_ref[...], wb_ref[0])
    acc = acc + _sigmoid(gb_ref[...]) * _dot(yb_ref[...], wb_ref[1])
    acc = acc + _sigmoid(gc_ref[...]) * _dot(yc_ref[...], wb_ref[2])
    o_ref[...] = acc.astype(o_ref.dtype)


def branch_merge(ya, yb, yc, w_branch, layer, z, tm, tn):
    M = ya.shape[0]
    nt = D_MODEL // tn
    g0 = GATE_COL0 // tn
    y_spec = pl.BlockSpec((tm, W_BR), lambda j, i: (i, 0))

    def gate_spec(b):
        return pl.BlockSpec((tm, tn), lambda j, i: (i, g0 + b * nt + j))

    return pl.pallas_call(
        _merge_kernel,
        grid=(nt, M // tm),
        in_specs=[y_spec, y_spec, y_spec,
                  pl.BlockSpec((None, N_BRANCH, W_BR, tn), lambda j, i: (layer, 0, 0, j)),
                  gate_spec(0), gate_spec(1), gate_spec(2)],
        out_specs=pl.BlockSpec((tm, tn), lambda j, i: (i, j)),
        out_shape=jax.ShapeDtypeStruct((M, D_MODEL), BF16),
        scratch_shapes=[pltpu.VMEM((N_BRANCH, W_BR, tn), BF16)],
        compiler_params=_params(("arbitrary", "arbitrary"), 48),
        name="branch_merge",
    )(ya, yb, yc, w_branch, z, z, z)


def _ln_store(v, g_ref, b_ref, o_ref, ob_ref):
    mu = jnp.mean(v, axis=-1, keepdims=True)
    c = v - mu
    var = jnp.mean(c * c, axis=-1, keepdims=True)
    o = c * lax.rsqrt(var + LN_EPS) * g_ref[...] + b_ref[...]
    o_ref[...] = o
    ob_ref[...] = o.astype(BF16)


def _ln_kernel(x_ref, y_ref, g_ref, b_ref, o_ref, ob_ref):
    _ln_store(ALPHA * x_ref[...] + y_ref[...], g_ref, b_ref, o_ref, ob_ref)


def residual_ln(x, y, g, b, layer, tm):
    M = x.shape[0]
    row = pl.BlockSpec((tm, D_MODEL), lambda i: (i, 0))
    vec = pl.BlockSpec((None, 1, D_MODEL), lambda i: (layer, 0, 0))
    return pl.pallas_call(
        _ln_kernel,
        grid=(M // tm,),
        in_specs=[row, row, vec, vec],
        out_specs=[row, row],
        out_shape=[jax.ShapeDtypeStruct((M, D_MODEL), F32), jax.ShapeDtypeStruct((M, D_MODEL), BF16)],
        compiler_params=_params(("arbitrary",), 48),
        name="residual_ln",
    )(x, y, g.reshape(DEPTH, 1, D_MODEL), b.reshape(DEPTH, 1, D_MODEL))


def _cast_kernel(x_ref, o_ref):
    o_ref[...] = x_ref[...].astype(BF16)


def to_bf16(x, tm):
    M, N = x.shape
    spec = pl.BlockSpec((tm, N), lambda i: (i, 0))
    return pl.pallas_call(_cast_kernel, grid=(M // tm,), in_specs=[spec], out_specs=spec,
                          out_shape=jax.ShapeDtypeStruct((M, N), BF16),
                          compiler_params=_params(("arbitrary",), 32), name="to_bf16")(x)


TOP_K = 2
MOE_TM = 512


def _top1(cols):
    m = cols[0]
    for c in cols[1:]:
        m = jnp.maximum(m, c)
    idx = jnp.full(m.shape, N_EXPERTS, jnp.int32)
    for e in reversed(range(N_EXPERTS)):
        idx = jnp.where(cols[e] == m, e, idx)
    return m, idx


def _router_kernel(x_ref, rt_ref, idx_ref, w_ref):
    x = x_ref[...]
    cols = [jnp.sum(x * rt_ref[e:e + 1, :], axis=-1, keepdims=True) for e in range(N_EXPERTS)]
    m1, i1 = _top1(cols)
    m2, i2 = _top1([jnp.where(i1 == e, -jnp.inf, cols[e]) for e in range(N_EXPERTS)])
    e2 = jnp.exp(m2 - m1)
    den = 1.0 + e2
    first = lax.broadcasted_iota(jnp.int32, idx_ref.shape, 1) == 0
    idx_ref[...] = jnp.where(first, i1, i2)
    w_ref[...] = jnp.where(first, 1.0 / den, e2 / den)


def router_top2(x, router_t, idx, tm):
    M = x.shape[0]
    out = pl.BlockSpec((tm, TOP_K), lambda i: (i, 0))
    return pl.pallas_call(
        _router_kernel,
        grid=(M // tm,),
        in_specs=[pl.BlockSpec((tm, D_MODEL), lambda i: (i, 0)),
                  pl.BlockSpec((None, N_EXPERTS, D_MODEL), lambda i: (idx, 0, 0))],
        out_specs=[out, out],
        out_shape=[jax.ShapeDtypeStruct((M, TOP_K), jnp.int32), jax.ShapeDtypeStruct((M, TOP_K), F32)],
        compiler_params=_params(("arbitrary",), 48),
        name="router",
    )(x, router_t)


def moe_schedule(idx, wts):
    n_pairs = idx.size
    nt = n_pairs // MOE_TM + N_EXPERTS
    n_slots = nt * MOE_TM
    e_flat = idx.reshape(-1)
    onehot = (e_flat[:, None] == jnp.arange(N_EXPERTS, dtype=jnp.int32)[None, :]).astype(jnp.int32)
    csum = jnp.cumsum(onehot, axis=0)
    rank = jnp.sum((csum - onehot) * onehot, axis=1)
    padded = (csum[-1] + (MOE_TM - 1)) // MOE_TM * MOE_TM
    gend = jnp.cumsum(padded)
    slot = (gend - padded)[e_flat] + rank
    pair = jnp.stack([jnp.arange(n_pairs, dtype=jnp.int32) // TOP_K,
                      lax.bitcast_convert_type(wts.reshape(-1), jnp.int32)], axis=1)
    placed = jnp.zeros((n_slots, 2), jnp.int32).at[slot].set(pair)
    src = placed[:, 0]
    wslot = lax.bitcast_convert_type(placed[:, 1:2], F32)
    n_used = gend[-1] // MOE_TM
    tstart = jnp.arange(nt, dtype=jnp.int32) * MOE_TM
    tstart = jnp.minimum(tstart, (n_used - 1) * MOE_TM)
    tile_e = jnp.sum((tstart[:, None] >= gend[None, :]).astype(jnp.int32), axis=1)
    return src, wslot, slot.astype(jnp.int32), tile_e.astype(jnp.int32), n_used.reshape(1).astype(jnp.int32)


def _row_copy(src_hbm, row, dst, i, sem):
    return pltpu.make_async_copy(src_hbm.at[pl.ds(row, 1)], dst.at[pl.ds(i, 1)], sem)


def _gather_cast_kernel(src_ref, nu_ref, x_hbm, o_ref, buf, sem):
    t = pl.program_id(0)
    n_used = nu_ref[0]
    cur = t % 2

    def issue(tile, b):
        def body(i, c):
            _row_copy(x_hbm, src_ref[tile * MOE_TM + i], buf.at[b], i, sem.at[b]).start()
            return c

        lax.fori_loop(0, MOE_TM, body, 0)

    @pl.when(t == 0)
    def _():
        issue(0, 0)

    @pl.when(t + 1 < n_used)
    def _():
        issue(t + 1, 1 - cur)

    @pl.when(t < n_used)
    def _():
        def drain(i, c):
            _row_copy(x_hbm, 0, buf.at[cur], i, sem.at[cur]).wait()
            return c

        lax.fori_loop(0, MOE_TM, drain, 0)
        o_ref[...] = buf[cur].astype(BF16)


def _used_tile(t, nu_ref):
    return jnp.minimum(t, nu_ref[0] - 1)


def moe_gather(x, src, n_used):
    n_slots = src.shape[0]
    return pl.pallas_call(
        _gather_cast_kernel,
        grid_spec=pltpu.PrefetchScalarGridSpec(
            num_scalar_prefetch=2,
            grid=(n_slots // MOE_TM,),
            in_specs=[pl.BlockSpec(memory_space=pl.ANY)],
            out_specs=pl.BlockSpec((MOE_TM, D_MODEL), lambda t, src, nu: (_used_tile(t, nu), 0)),
            scratch_shapes=[pltpu.VMEM((2, MOE_TM, D_MODEL), F32), pltpu.SemaphoreType.DMA((2,))]),
        out_shape=jax.ShapeDtypeStruct((n_slots, D_MODEL), BF16),
        compiler_params=_params(("arbitrary",), 40),
        name="moe_gather",
    )(src, n_used, x)


def _new_expert(te_ref, t):
    return (t == 0) | (te_ref[t] != te_ref[jnp.maximum(t - 1, 0)])


def _moe_up_kernel(te_ref, nu_ref, x_ref, w1_ref, w3_ref, o_ref, wb_ref, *, tn):
    t = pl.program_id(1)

    @pl.when(_new_expert(te_ref, t))
    def _():
        wb_ref[:, :tn] = w1_ref[...].astype(BF16)
        wb_ref[:, tn:] = w3_ref[...].astype(BF16)

    @pl.when(t < nu_ref[0])
    def _():
        ab = _dot(x_ref[...], wb_ref[...])
        o_ref[...] = (_silu(ab[:, :tn]) * ab[:, tn:]).astype(o_ref.dtype)


def moe_up(xs, w1, w3, layer, tile_e, n_used, tn):
    n_slots, K = xs.shape
    FF = w1.shape[-1]
    wspec = pl.BlockSpec((None, None, K, tn), lambda j, t, te, nu: (layer, te[t], 0, j))
    return pl.pallas_call(
        functools.partial(_moe_up_kernel, tn=tn),
        grid_spec=pltpu.PrefetchScalarGridSpec(
            num_scalar_prefetch=2,
            grid=(pl.cdiv(FF, tn), n_slots // MOE_TM),
            in_specs=[pl.BlockSpec((MOE_TM, K), lambda j, t, te, nu: (_used_tile(t, nu), 0)), wspec, wspec],
            out_specs=pl.BlockSpec((MOE_TM, tn), lambda j, t, te, nu: (_used_tile(t, nu), j)),
            scratch_shapes=[pltpu.VMEM((K, 2 * tn), BF16)]),
        out_shape=jax.ShapeDtypeStruct((n_slots, FF), BF16),
        compiler_params=_params(("arbitrary", "arbitrary"), 56),
        name="moe_up",
    )(tile_e, n_used, xs, w1, w3)


def _moe_down_kernel(te_ref, nu_ref, h_ref, w_ref, g_ref, o_ref, wb_ref):
    t = pl.program_id(1)

    @pl.when(_new_expert(te_ref, t))
    def _():
        wb_ref[...] = w_ref[...].astype(BF16)

    @pl.when(t < nu_ref[0])
    def _():
        o_ref[...] = g_ref[...] * _dot(h_ref[...], wb_ref[...])


def moe_down(hs, w2, layer, wslot, tile_e, n_used, tn):
    n_slots, K = hs.shape
    N = w2.shape[-1]
    return pl.pallas_call(
        _moe_down_kernel,
        grid_spec=pltpu.PrefetchScalarGridSpec(
            num_scalar_prefetch=2,
            grid=(N // tn, n_slots // MOE_TM),
            in_specs=[pl.BlockSpec((MOE_TM, K), lambda j, t, te, nu: (_used_tile(t, nu), 0)),
                      pl.BlockSpec((None, None, K, tn), lambda j, t, te, nu: (layer, te[t], 0, j)),
                      pl.BlockSpec((MOE_TM, 1), lambda j, t, te, nu: (_used_tile(t, nu), 0))],
            out_specs=pl.BlockSpec((MOE_TM, tn), lambda j, t, te, nu: (_used_tile(t, nu), j)),
            scratch_shapes=[pltpu.VMEM((K, tn), BF16)]),
        out_shape=jax.ShapeDtypeStruct((n_slots, N), F32),
        compiler_params=_params(("arbitrary", "arbitrary"), 56),
        name="moe_down",
    )(tile_e, n_used, hs, w2, wslot)


def _ln_moe_kernel(slot_ref, x_ref, eo_hbm, g_ref, b_ref, o_ref, ob_ref, ya, yb, sem, *, tm):
    t = pl.program_id(0)
    cur = t % 2

    def issue(tile, b):
        def body(i, c):
            p = TOP_K * (tile * tm + i)
            _row_copy(eo_hbm, slot_ref[p], ya.at[b], i, sem.at[b]).start()
            _row_copy(eo_hbm, slot_ref[p + 1], yb.at[b], i, sem.at[b]).start()
            return c

        lax.fori_loop(0, tm, body, 0)

    @pl.when(t == 0)
    def _():
        issue(0, 0)

    @pl.when(t + 1 < pl.num_programs(0))
    def _():
        issue(t + 1, 1 - cur)

    def drain(i, c):
        _row_copy(eo_hbm, 0, ya.at[cur], i, sem.at[cur]).wait()
        _row_copy(eo_hbm, 0, yb.at[cur], i, sem.at[cur]).wait()
        return c

    lax.fori_loop(0, tm, drain, 0)
    _ln_store(ALPHA * x_ref[...] + (ya[cur] + yb[cur]), g_ref, b_ref, o_ref, ob_ref)


def residual_ln_moe(x, eo, slot, g, b, layer, tm):
    M = x.shape[0]
    row = pl.BlockSpec((tm, D_MODEL), lambda i, s: (i, 0))
    vec = pl.BlockSpec((None, 1, D_MODEL), lambda i, s: (layer, 0, 0))
    return pl.pallas_call(
        functools.partial(_ln_moe_kernel, tm=tm),
        grid_spec=pltpu.PrefetchScalarGridSpec(
            num_scalar_prefetch=1,
            grid=(M // tm,),
            in_specs=[row, pl.BlockSpec(memory_space=pl.ANY), vec, vec],
            out_specs=[row, row],
            scratch_shapes=[pltpu.VMEM((2, tm, D_MODEL), F32), pltpu.VMEM((2, tm, D_MODEL), F32),
                            pltpu.SemaphoreType.DMA((2,))]),
        out_shape=[jax.ShapeDtypeStruct((M, D_MODEL), F32), jax.ShapeDtypeStruct((M, D_MODEL), BF16)],
        compiler_params=_params(("arbitrary",), 48),
        name="residual_ln_moe",
    )(slot, x, eo, g.reshape(DEPTH, 1, D_MODEL), b.reshape(DEPTH, 1, D_MODEL))


def _hgrn_same_subblock(q, k, v, b):
    SB = GLA_SUB
    half = SB // 2
    rid = lax.broadcasted_iota(jnp.int32, (SB, HEAD), 0)
    rid_hi = lax.broadcasted_iota(jnp.int32, (half, HEAD), 0) + half
    outs = []
    for i in range(q.shape[0] // SB):
        lo = i * SB
        qi, bi, ki, vi = q[lo:lo + SB], b[lo:lo + SB], k[lo:lo + SB], v[lo:lo + SB]
        acc = jnp.zeros((SB, HEAD), F32)
        acc_hi = jnp.zeros((half, HEAD), F32)
        for s in range(half):
            d = jnp.where(rid >= s, bi - bi[s:s + 1], NEG_BIG)
            w = qi * jnp.exp2(d) * ki[s:s + 1]
            acc = acc + jnp.sum(w, axis=1, keepdims=True) * vi[s:s + 1]
        qh, bh = qi[half:], bi[half:]
        for s in range(half, SB):
            d = jnp.where(rid_hi >= s, bh - bi[s:s + 1], NEG_BIG)
            w = qh * jnp.exp2(d) * ki[s:s + 1]
            acc_hi = acc_hi + jnp.sum(w, axis=1, keepdims=True) * vi[s:s + 1]
        outs.append(acc[:half])
        outs.append(acc[half:] + acc_hi)
    return jnp.concatenate(outs, axis=0)


def _hgrn_prompt_kernel(q_ref, f_ref, i_ref, g_ref, lb_ref, y_ref, s_ref, st_ref, *, tb, hp):
    t = pl.program_id(2)

    @pl.when(t == 0)
    def _():
        st_ref[...] = jnp.zeros_like(st_ref)

    C, SB = GLA_CHUNK, GLA_SUB
    nc = tb // C
    lb = lb_ref[...]
    f = lb + (1.0 - lb) * _sigmoid(f_ref[...])
    r = lax.broadcasted_iota(jnp.int32, (tb, tb), 0)
    c = lax.broadcasted_iota(jnp.int32, (tb, tb), 1)
    tril = ((r >= c) & (r // C == c // C)).astype(F32)
    b_all = jnp.dot(tril, jnp.log2(f), preferred_element_type=F32, precision=lax.Precision.HIGHEST)
    k_all = 1.0 - f
    same_chunk = r // C == c // C
    rid = lax.broadcasted_iota(jnp.int32, (tb, HEAD), 0)
    sub = (rid % C) // SB
    chunk = rid // C

    def row_bcast(x, row_of_chunk):
        return jnp.concatenate([jnp.broadcast_to(x[ci * C + row_of_chunk:ci * C + row_of_chunk + 1], (C, HEAD))
                                for ci in range(nc)], axis=0)

    for h in range(hp):
        cols = slice(h * HEAD, (h + 1) * HEAD)
        q, v = q_ref[:, cols], i_ref[:, cols]
        k, b = k_all[:, cols], b_all[:, cols]
        vb = v.astype(BF16)
        o = _hgrn_same_subblock(q, k, v, b)

        betas = [row_bcast(b, i * SB - 1) for i in range(1, C // SB)]
        beta_row = betas[-1]
        for i in range(C // SB - 2, 0, -1):
            beta_row = jnp.where(sub == i, betas[i - 1], beta_row)
        qd = q * jnp.exp2(jnp.where(sub > 0, b - beta_row, NEG_BIG))
        q_parts = [jnp.where(sub == i, qd, 0.0) for i in range(1, C // SB)]
        k_parts = [k * jnp.exp2(jnp.where(sub < i, betas[i - 1] - b, NEG_BIG)) for i in range(1, C // SB)]
        scores = _dot_nt(jnp.concatenate(q_parts, axis=1).astype(BF16), jnp.concatenate(k_parts, axis=1).astype(BF16))
        o = o + _dot(jnp.where(same_chunk, scores, 0.0).astype(BF16), vb)

        b_last = row_bcast(b, C - 1)
        kt = (k * jnp.exp2(b_last - b)).astype(BF16)
        zero = jnp.zeros_like(kt)
        upd = _dot_tn(vb, jnp.concatenate([jnp.where(chunk == ci, kt, zero) for ci in range(nc)], axis=1))
        st = st_ref[h]
        starts = []
        for ci in range(nc):
            starts.append(st.astype(BF16))
            st = jnp.exp2(b[ci * C + C - 1:ci * C + C]) * st + upd[:, ci * HEAD:(ci + 1) * HEAD]
        st_ref[h] = st
        inter = _dot_nt((q * jnp.exp2(b)).astype(BF16), jnp.concatenate(starts, axis=0))
        o = o + jnp.concatenate([inter[ci * C:(ci + 1) * C, ci * HEAD:(ci + 1) * HEAD] for ci in range(nc)], axis=0)
        y_ref[:, cols] = (_head_rms(o) * _silu(g_ref[:, cols])).astype(BF16)

    @pl.when(t == pl.num_programs(2) - 1)
    def _():
        for h in range(hp):
            s_ref[h] = st_ref[h].T


def hgrn_prompt(z, lb, nb, T, tb, hp, m_total):
    nt = T // tb
    ng = N_HEADS // hp

    def seg(s):
        return pl.BlockSpec((tb, hp * HEAD), lambda b, h, t: (b * nt + t, s * ng + h))

    return pl.pallas_call(
        functools.partial(_hgrn_prompt_kernel, tb=tb, hp=hp),
        grid=(nb, ng, nt),
        in_specs=[seg(SEG_QA), seg(SEG_FA), seg(SEG_IA), seg(SEG_GA),
                  pl.BlockSpec((1, hp * HEAD), lambda b, h, t: (0, h))],
        out_specs=[pl.BlockSpec((tb, hp * HEAD), lambda b, h, t: (b * nt + t, h)),
                   pl.BlockSpec((None, hp, HEAD, HEAD), lambda b, h, t: (b, h, 0, 0))],
        out_shape=[jax.ShapeDtypeStruct((m_total, W_BR), BF16),
                   jax.ShapeDtypeStruct((nb, N_HEADS, HEAD, HEAD), F32)],
        scratch_shapes=[pltpu.VMEM((hp, HEAD, HEAD), F32)],
        compiler_params=_params(("arbitrary", "arbitrary", "arbitrary"), 32),
        name="hgrn_prompt",
    )(z, z, z, z, lb)


def _rotate(x, cosf, sinf):
    return x * cosf + pltpu.roll(x, HEAD // 2, 1) * sinf


def _ret_prompt_kernel(q_ref, k_ref, v_ref, g_ref, cos_ref, sin_ref, lg_ref, y_ref, s_ref, st_ref, *, tb, hp):
    t = pl.program_id(2)

    @pl.when(t == 0)
    def _():
        st_ref[...] = jnp.zeros_like(st_ref)

    cosf, sinf = cos_ref[...], sin_ref[...]
    tpos = lax.broadcasted_iota(jnp.int32, (tb, HEAD), 0).astype(F32)
    r = lax.broadcasted_iota(jnp.int32, (tb, tb), 0)
    c = lax.broadcasted_iota(jnp.int32, (tb, tb), 1)
    causal = r >= c
    lag = (r - c).astype(F32)
    for h in range(hp):
        cols = slice(h * HEAD, (h + 1) * HEAD)
        lg_w = lg_ref[h]
        lg = lg_w[:, :HEAD]
        qr = _rotate(q_ref[:, cols], cosf, sinf)
        kr = _rotate(k_ref[:, cols], cosf, sinf) * (HEAD ** -0.5)
        vb = v_ref[:, cols].astype(BF16)
        dmat = jnp.where(causal, jnp.exp(lag * lg_w), 0.0)
        st = st_ref[h]
        scores = _dot_nt(qr.astype(BF16), kr.astype(BF16)) * dmat
        o = _dot(scores.astype(BF16), vb)
        o = o + _dot_nt((qr * jnp.exp((tpos + 1.0) * lg)).astype(BF16), st.astype(BF16))
        kt = (kr * jnp.exp((tb - 1.0 - tpos) * lg)).astype(BF16)
        st_ref[h] = jnp.exp(float(tb) * lg) * st + _dot_tn(vb, kt)
        y_ref[:, cols] = (_head_rms(o) * _silu(g_ref[:, cols])).astype(BF16)

    @pl.when(t == pl.num_programs(2) - 1)
    def _():
        for h in range(hp):
            s_ref[h] = st_ref[h].T


def _rope_tables(pos):
    half = HEAD // 2
    inv_freq = 1.0 / (ROPE_BASE ** jnp.linspace(0.0, 1.0, half, dtype=F32))
    ang = pos.astype(F32)[:, None] * inv_freq[None, :]
    cos, sin = jnp.cos(ang), jnp.sin(ang)
    return jnp.concatenate([cos, cos], axis=-1), jnp.concatenate([-sin, sin], axis=-1)


def _log_gamma():
    return jnp.log1p(-jnp.power(2.0, -5.0 - jnp.arange(N_HEADS, dtype=F32)))


def ret_prompt(z, nb, T, tb, hp, m_total):
    nt = T // tb
    ng = N_HEADS // hp
    cosf, sinf = _rope_tables(jnp.arange(T))
    lg = jnp.broadcast_to(_log_gamma()[:, None, None], (N_HEADS, 1, tb))

    def seg(s):
        return pl.BlockSpec((tb, hp * HEAD), lambda b, h, t: (b * nt + t, s * ng + h))

    tab = pl.BlockSpec((tb, HEAD), lambda b, h, t: (t, 0))
    return pl.pallas_call(
        functools.partial(_ret_prompt_kernel, tb=tb, hp=hp),
        grid=(nb, ng, nt),
        in_specs=[seg(SEG_QC), seg(SEG_KC), seg(SEG_VC), seg(SEG_GC), tab, tab,
                  pl.BlockSpec((hp, 1, tb), lambda b, h, t: (h, 0, 0))],
        out_specs=[pl.BlockSpec((tb, hp * HEAD), lambda b, h, t: (b * nt + t, h)),
                   pl.BlockSpec((None, hp, HEAD, HEAD), lambda b, h, t: (b, h, 0, 0))],
        out_shape=[jax.ShapeDtypeStruct((m_total, W_BR), BF16),
                   jax.ShapeDtypeStruct((nb, N_HEADS, HEAD, HEAD), F32)],
        scratch_shapes=[pltpu.VMEM((hp, HEAD, HEAD), F32)],
        compiler_params=_params(("arbitrary", "arbitrary", "arbitrary"), 32),
        name="ret_prompt",
    )(z, z, z, z, cosf, sinf, lg)


def _lru_gates(xc, wa_ref, ba_ref, wx_ref, bx_ref, lam_ref):
    xcb = xc.astype(BF16)
    ra, ia = [], []
    for n in range(W_BR // HEAD):
        xn = xcb[:, n * HEAD:(n + 1) * HEAD]
        ra.append(_dot(xn, wa_ref[n].astype(BF16)))
        ia.append(_dot(xn, wx_ref[n].astype(BF16)))
    r = _sigmoid(jnp.concatenate(ra, axis=1) + ba_ref[...])
    i = _sigmoid(jnp.concatenate(ia, axis=1) + bx_ref[...])
    nlam = -lam_ref[...]
    softplus = jnp.maximum(nlam, 0.0) + jnp.log1p(jnp.exp(-jnp.abs(nlam)))
    log_a = -LRU_C * r * softplus
    a = jnp.exp(log_a)
    th = jnp.tanh(log_a)
    one_m_a2 = -2.0 * th / (1.0 - th)
    root = jnp.where(one_m_a2 > 0.0, one_m_a2 * lax.rsqrt(one_m_a2), 0.0)
    u = root * (i * xc)
    return a, u


def _lru_prompt_kernel(x_ref, g_ref, cw_ref, cb_ref, wa_ref, ba_ref, wx_ref, bx_ref, lam_ref,
                       y_ref, h_ref, conv_ref, prev_ref, hc_ref, *, tb):
    t = pl.program_id(1)

    @pl.when(t == 0)
    def _():
        prev_ref[...] = jnp.zeros_like(prev_ref)
        hc_ref[...] = jnp.zeros_like(hc_ref)

    x = x_ref[...]
    prev = prev_ref[...]
    rid = lax.broadcasted_iota(jnp.int32, (tb, W_BR), 0)
    rid8 = lax.broadcasted_iota(jnp.int32, (SUBLANE, W_BR), 0)
    cw = cw_ref[...]
    xc = cb_ref[...] + cw[CONV_W - 1:CONV_W] * x
    for j in range(1, CONV_W):
        xr = pltpu.roll(x, j, 0)
        head = jnp.where(rid8 < j, pltpu.roll(prev, j, 0), xr[:SUBLANE])
        xc = xc + cw[CONV_W - 1 - j:CONV_W - j] * jnp.concatenate([head, xr[SUBLANE:]], axis=0)
    a, u = _lru_gates(xc, wa_ref, ba_ref, wx_ref, bx_ref, lam_ref)
    row_in_group = rid % SUBLANE
    d = 1
    while d < SUBLANE:
        keep = row_in_group >= d
        a_s = jnp.where(keep, pltpu.roll(a, d, 0), 1.0)
        u_s = jnp.where(keep, pltpu.roll(u, d, 0), 0.0)
        u = a * u_s + u
        a = a * a_s
        d *= 2
    carry = hc_ref[0:1, :]
    groups = []
    for g in range(tb // SUBLANE):
        rows = slice(g * SUBLANE, (g + 1) * SUBLANE)
        hg = a[rows] * carry + u[rows]
        groups.append(hg)
        carry = hg[SUBLANE - 1:SUBLANE]
    h = jnp.concatenate(groups, axis=0)
    y_ref[...] = (h * _gelu_tanh(g_ref[...])).astype(BF16)
    hc_ref[0:1, :] = h[tb - 1:tb]
    prev_ref[...] = x[tb - SUBLANE:tb]

    @pl.when(t == pl.num_programs(1) - 1)
    def _():
        h_ref[...] = h[tb - 1:tb]
        conv_ref[...] = x[tb - (CONV_W - 1):tb]


def _lru_weight_specs(layer):
    def vec():
        return pl.BlockSpec((None, 1, W_BR), lambda *_: (layer, 0, 0))

    def blk():
        return pl.BlockSpec((None, W_BR // HEAD, HEAD, HEAD), lambda *_: (layer, 0, 0, 0))

    return [pl.BlockSpec((None, CONV_W, W_BR), lambda *_: (layer, 0, 0)), vec(), blk(), vec(), blk(), vec(), vec()]


def _lru_weight_args(p):
    v = lambda a: a.reshape(DEPTH, 1, W_BR)
    return [p["lru_conv_w"], v(p["lru_conv_b"]), p["lru_wa"], v(p["lru_ba"]), p["lru_wx"], v(p["lru_bx"]),
            v(p["lru_lambda"])]


def lru_prompt(z, p, layer, nb, T, tb, m_total):
    nt = T // tb
    outs = pl.pallas_call(
        functools.partial(_lru_prompt_kernel, tb=tb),
        grid=(nb, nt),
        in_specs=[pl.BlockSpec((tb, W_BR), lambda b, t: (b * nt + t, SEG_XB)),
                  pl.BlockSpec((tb, W_BR), lambda b, t: (b * nt + t, SEG_GB))] + _lru_weight_specs(layer),
        out_specs=[pl.BlockSpec((tb, W_BR), lambda b, t: (b * nt + t, 0)),
                   pl.BlockSpec((None, 1, W_BR), lambda b, t: (b, 0, 0)),
                   pl.BlockSpec((None, CONV_W - 1, W_BR), lambda b, t: (b, 0, 0))],
        out_shape=[jax.ShapeDtypeStruct((m_total, W_BR), BF16),
                   jax.ShapeDtypeStruct((nb, 1, W_BR), F32),
                   jax.ShapeDtypeStruct((nb, CONV_W - 1, W_BR), F32)],
        scratch_shapes=[pltpu.VMEM((SUBLANE, W_BR), F32), pltpu.VMEM((SUBLANE, W_BR), F32)],
        compiler_params=_params(("arbitrary", "arbitrary"), 48),
        name="lru_prompt",
    )(z, z, *_lru_weight_args(p))
    return outs[0], outs[1].reshape(nb, W_BR), outs[2]


def _lru_sample_kernel(x_ref, g_ref, buf_ref, h0_ref, cw_ref, cb_ref, wa_ref, ba_ref, wx_ref, bx_ref, lam_ref,
                       y_in_ref, y_ref, h_ref, conv_ref):
    del y_in_ref
    x = x_ref[...]
    cw = cw_ref[...]
    xc = cb_ref[...] + cw[CONV_W - 1:CONV_W] * x
    for j in range(CONV_W - 1):
        xc = xc + cw[j:j + 1] * buf_ref[j]
    a, u = _lru_gates(xc, wa_ref, ba_ref, wx_ref, bx_ref, lam_ref)
    h = a * h0_ref[...] + u
    y_ref[...] = (h * _gelu_tanh(g_ref[...])).astype(BF16)
    h_ref[...] = h
    for j in range(CONV_W - 2):
        conv_ref[j] = buf_ref[j + 1]
    conv_ref[CONV_W - 2] = x


def lru_sample(z, p, layer, row_blk, ns, buf_t, h0, y):
    full3 = pl.BlockSpec((CONV_W - 1, ns, W_BR), lambda i: (0, 0, 0))
    full2 = pl.BlockSpec((ns, W_BR), lambda i: (0, 0))
    n_in = 4 + len(_lru_weight_specs(layer))
    return pl.pallas_call(
        _lru_sample_kernel,
        grid=(1,),
        in_specs=[pl.BlockSpec((ns, W_BR), lambda i: (row_blk, SEG_XB)),
                  pl.BlockSpec((ns, W_BR), lambda i: (row_blk, SEG_GB)), full3, full2] + _lru_weight_specs(layer)
                 + [pl.BlockSpec(memory_space=pl.ANY)],
        out_specs=[pl.BlockSpec((ns, W_BR), lambda i: (row_blk, 0)), full2, full3],
        out_shape=[jax.ShapeDtypeStruct(y.shape, BF16), jax.ShapeDtypeStruct((ns, W_BR), F32),
                   jax.ShapeDtypeStruct((CONV_W - 1, ns, W_BR), F32)],
        input_output_aliases={n_in: 0},
        compiler_params=_params(("arbitrary",), 32),
        name="lru_sample",
    )(z, z, buf_t, h0, *_lru_weight_args(p), y)


GRP = SUBLANE


def _channel_major(zs, seg):
    ns = zs.shape[0]
    x = zs[:, seg * W_BR:(seg + 1) * W_BR].reshape(ns // GRP, GRP, N_HEADS, HEAD)
    return x.transpose(0, 2, 3, 1)


def _hgrn_sample_kernel(f_ref, lb_ref, q_ref, v_ref, g_ref, s_ref, *aliased_and_outs):
    y_ref, so_ref, o_scr = aliased_and_outs[-3:]
    for h in range(N_HEADS):
        cols = slice(h * HEAD, (h + 1) * HEAD)
        lb = lb_ref[h]
        f = lb + (1.0 - lb) * _sigmoid(f_ref[h])
        qb = q_ref[:, cols].astype(BF16)
        for j in range(GRP):
            vrow = v_ref[j:j + 1, cols]
            sn = f[:, j:j + 1] * (s_ref[j, h] - vrow) + vrow
            so_ref[j, h] = sn
            o_scr[j:j + 1, cols] = _dot(qb, sn.astype(BF16))[j:j + 1]
    for h in range(N_HEADS):
        cols = slice(h * HEAD, (h + 1) * HEAD)
        y_ref[:, cols] = (_head_rms(o_scr[:, cols]) * _silu(g_ref[:, cols])).astype(BF16)


def _sample_state_call(body, name, lead_args, lead_specs, z, segs, states, layer, row0, y, stacked):
    ns = states.shape[1]
    rb = row0 // GRP
    st = pl.BlockSpec((None, GRP, N_HEADS, HEAD, HEAD), lambda g: (layer, g, 0, 0, 0))
    anyspec = pl.BlockSpec(memory_space=pl.ANY)
    in_specs = lead_specs + [pl.BlockSpec((GRP, W_BR), lambda g, s=s: (rb + g, s)) for s in segs] + [st, anyspec]
    args = lead_args + [z] * len(segs) + [states, y]
    alias = {len(args) - 1: 0}
    if stacked is not None:
        in_specs.append(anyspec)
        args.append(stacked)
        alias[len(args) - 1] = 1
    return pl.pallas_call(
        body,
        grid=(ns // GRP,),
        in_specs=in_specs,
        out_specs=[pl.BlockSpec((GRP, W_BR), lambda g: (rb + g, 0)), st],
        out_shape=[jax.ShapeDtypeStruct(y.shape, BF16), jax.ShapeDtypeStruct(states.shape, F32)],
        scratch_shapes=[pltpu.VMEM((GRP, W_BR), F32)],
        input_output_aliases=alias,
        compiler_params=_params(("arbitrary",), 48),
        name=name,
    )(*args)


def hgrn_sample(z, zs, lb, states, layer, row0, y, stacked):
    cm = pl.BlockSpec((None, N_HEADS, HEAD, GRP), lambda g: (g, 0, 0, 0))
    lead_specs = [cm, pl.BlockSpec((N_HEADS, HEAD, 1), lambda g: (0, 0, 0))]
    lead_args = [_channel_major(zs, SEG_FA), lb.reshape(N_HEADS, HEAD, 1)]
    return _sample_state_call(_hgrn_sample_kernel, "hgrn_sample", lead_args, lead_specs, z,
                              (SEG_QA, SEG_IA, SEG_GA), states, layer, row0, y, stacked)


def _ret_sample_kernel(k_ref, cos_ref, sin_ref, cosf_ref, sinf_ref, gam_ref, q_ref, v_ref, g_ref, s_ref,
                       *aliased_and_outs):
    y_ref, so_ref, o_scr = aliased_and_outs[-3:]
    half = HEAD // 2
    cosc, sinc = cos_ref[...], sin_ref[...]
    cosf, sinf = cosf_ref[...], sinf_ref[...]

    def rot(x):
        x1, x2 = x[:half], x[half:]
        return jnp.concatenate([x1 * cosc - x2 * sinc, x1 * sinc + x2 * cosc], axis=0)

    for h in range(N_HEADS):
        cols = slice(h * HEAD, (h + 1) * HEAD)
        k = rot(k_ref[h]) * (HEAD ** -0.5)
        qb = _rotate(q_ref[:, cols], cosf, sinf).astype(BF16)
        gam = gam_ref[h]
        for j in range(GRP):
            sn = gam * s_ref[j, h] + k[:, j:j + 1] * v_ref[j:j + 1, cols]
            so_ref[j, h] = sn
            o_scr[j:j + 1, cols] = _dot(qb, sn.astype(BF16))[j:j + 1]
    for h in range(N_HEADS):
        cols = slice(h * HEAD, (h + 1) * HEAD)
        y_ref[:, cols] = (_head_rms(o_scr[:, cols]) * _silu(g_ref[:, cols])).astype(BF16)


def ret_sample(z, zs, states, layer, row0, y, stacked):
    half = HEAD // 2
    cosf, sinf = _rope_tables(jnp.full((1,), PAST_LEN))
    cosc = cosf[0, :half].reshape(half, 1)
    sinc = sinf[0, half:].reshape(half, 1)
    gam = jnp.broadcast_to(jnp.exp(_log_gamma())[:, None, None], (N_HEADS, 1, HEAD))
    cm = pl.BlockSpec((None, N_HEADS, HEAD, GRP), lambda g: (g, 0, 0, 0))
    col = pl.BlockSpec((half, 1), lambda g: (0, 0))
    row = pl.BlockSpec((1, HEAD), lambda g: (0, 0))
    lead_specs = [cm, col, col, row, row, pl.BlockSpec((N_HEADS, 1, HEAD), lambda g: (0, 0, 0))]
    lead_args = [_channel_major(zs, SEG_KC), cosc, sinc, cosf, sinf, gam]
    return _sample_state_call(_ret_sample_kernel, "ret_sample", lead_args, lead_specs, z,
                              (SEG_QC, SEG_VC, SEG_GC), states, layer, row0, y, stacked)


TM = 1040
TN_STAGED = 1024
TM_DOWN = 520
TM_ROW = 208
TB_SEQ = 256
HEADS_PER_STEP = 8


def kernel(x_prompt, x_sample, state_hgrn, state_lru_h, state_lru_conv, state_ret, w_in, hgrn_lb_logits, lru_conv_w, lru_conv_b, lru_wa, lru_ba, lru_wx, lru_bx, lru_lambda, w_branch, w_out, ln1_g, ln1_b, ln2_g, ln2_b, ffn_w1, ffn_w3, ffn_w2, router, moe_w1, moe_w3, moe_w2):
    nb, T, _ = x_prompt.shape
    ns = x_sample.shape[0]
    mp = nb * T
    lru_p = dict(lru_conv_w=lru_conv_w, lru_conv_b=lru_conv_b, lru_wa=lru_wa, lru_ba=lru_ba,
                 lru_wx=lru_wx, lru_bx=lru_bx, lru_lambda=lru_lambda)

    lb_all = hgrn_lower_bounds(hgrn_lb_logits)
    x = jnp.concatenate([x_prompt.reshape(mp, D_MODEL), x_sample.reshape(ns, D_MODEL)], axis=0)
    xb = to_bf16(x, TM_ROW)
    router_t = router.transpose(0, 2, 1)

    m_total = mp + ns
    new_p, new_s = [], []
    hgrn_s = ret_s = None
    for l in range(DEPTH):
        z = matmul(xb, w_in, l, TM, TN_STAGED, name="in_proj")
        zs = z[mp:]
        lb = lb_all[l:l + 1]

        ya, hgrn_p = hgrn_prompt(z, lb, nb, T, TB_SEQ, HEADS_PER_STEP, m_total)
        yb, lru_h_p, conv_p = lru_prompt(z, lru_p, l, nb, T, TB_SEQ, m_total)
        yc, ret_p = ret_prompt(z, nb, T, TB_SEQ, HEADS_PER_STEP, m_total)

        ya, hgrn_s = hgrn_sample(z, zs, lb, state_hgrn, l, mp, ya, hgrn_s)
        yb, lru_h_s, conv_s = lru_sample(z, lru_p, l, mp // ns, ns, state_lru_conv[l].transpose(1, 0, 2),
                                         state_lru_h[l], yb)
        yc, ret_s = ret_sample(z, zs, state_ret, l, mp, yc, ret_s)
        new_p.append((hgrn_p, lru_h_p, conv_p, ret_p))
        new_s.append((lru_h_s, conv_s.transpose(1, 0, 2)))

        merged = branch_merge(ya, yb, yc, w_branch, l, z, TM, 512)
        mix = matmul(merged, w_out, l, TM, TN_STAGED, name="out_proj")
        x, xb = residual_ln(x, mix, ln1_g, ln1_b, l, TM_ROW)

        if l % 2 == 0:
            h = swiglu_up(xb, ffn_w1, ffn_w3, (l // 2,), TM, 256, name="ffn_up")
            y = down_proj(h, ffn_w2, l // 2, 0, TM_DOWN, 512, name="ffn_down")
            y = down_proj(h, ffn_w2, l // 2, 1, TM_DOWN, 512, prev=y, name="ffn_down")
            x, xb = residual_ln(x, y, ln2_g, ln2_b, l, TM_ROW)
        else:
            eidx, ew = router_top2(x, router_t, l // 2, TM_DOWN)
            src, wslot, slot, tile_e, n_used = moe_schedule(eidx, ew)
            xs = moe_gather(x, src, n_used)
            hs = moe_up(xs, moe_w1, moe_w3, l // 2, tile_e, n_used, 512)
            eo = moe_down(hs, moe_w2, l // 2, wslot, tile_e, n_used, 512)
            x, xb = residual_ln_moe(x, eo, slot, ln2_g, ln2_b, l, TM_ROW)

    def stack(states, j):
        return jnp.stack([s[j] for s in states], axis=0)

    return (x[:mp].reshape(nb, T, D_MODEL), x[mp:].reshape(ns, 1, D_MODEL),
            stack(new_p, 0), stack(new_p, 1), stack(new_p, 2), stack(new_p, 3),
            hgrn_s, stack(new_s, 0), stack(new_s, 1), ret_s)
```
